```python
import math
import jax, jax.numpy as jnp
from jax import lax
import numpy as np

D_MODEL = 2048
BATCH = 2
SEQ = 4096
DEPTH = 2

GRID_W = 64
CTX_LEN = 256
HEAD_DIM = 128
N_HEADS_NA = 6
N_HEADS_MLA = 5
N_HEADS_DIFF = 5
NA_KH = 8
NA_KW = 16
MLA_Q_RANK = 768
MLA_KV_RANK = 512
MLA_NOPE_DIM = 128
MLA_ROPE_DIM = 64
MLA_V_DIM = 128
DIFF_QK_DIM = HEAD_DIM // 2
FFN_DIM = 5632
N_MOD = 9
ROPE_THETA = 10000.0
NORM_EPS = 1e-6
Q_BLOCK = 128
NEG_INF = -1e30
NA_W = N_HEADS_NA * HEAD_DIM
MLA_W = N_HEADS_MLA * MLA_V_DIM
DIFF_W = N_HEADS_DIFF * HEAD_DIM
MIX_W = NA_W + MLA_W + DIFF_W
IN_SPLITS = (NA_W, NA_W, NA_W, MLA_Q_RANK, MLA_KV_RANK, MLA_ROPE_DIM, DIFF_W, DIFF_W, DIFF_W, D_MODEL, D_MODEL, D_MODEL)
IN_W = sum(IN_SPLITS)
NA_SCALE = HEAD_DIM ** -0.5
MLA_SCALE = (MLA_NOPE_DIM + MLA_ROPE_DIM) ** -0.5
DIFF_SCALE = DIFF_QK_DIM ** -0.5

kernel_name = "hybrid_dit_na_mla_diffattn_macaron"


def rmsnorm(x, g):
    xf = x.astype(jnp.float32)
    y = xf * lax.rsqrt(jnp.mean(xf * xf, axis=-1, keepdims=True) + NORM_EPS)
    return (y * g.astype(jnp.float32)).astype(x.dtype)


def modulate(x, g, shift, scale):
    return rmsnorm(x, g) * (1 + scale) + shift


def swiglu(h, w_in, w_out):
    a, b = jnp.split(h @ w_in, 2, axis=-1)
    return (jax.nn.silu(a) * b) @ w_out


def split_cols(p, sizes):
    return jnp.split(p, [int(i) for i in np.cumsum(sizes)[:-1]], axis=-1)


def to_heads(t, n_heads):
    return t.reshape(*t.shape[:2], n_heads, -1)


def rope_2d(x, rows, cols):
    half = x.shape[-1] // 2
    quarter = half // 2
    freqs = ROPE_THETA ** (-jnp.arange(quarter, dtype=jnp.float32) / quarter)

    def rotate(xp, pos):
        ang = pos.astype(jnp.float32)[:, None] * freqs
        cos = jnp.cos(ang)[:, None, :].astype(xp.dtype)
        sin = jnp.sin(ang)[:, None, :].astype(xp.dtype)
        x1, x2 = xp[..., :quarter], xp[..., quarter:]
        return jnp.concatenate([x1 * cos - x2 * sin, x1 * sin + x2 * cos], axis=-1)

    return jnp.concatenate([rotate(x[..., :half], rows), rotate(x[..., half:], cols)], axis=-1)


def scores(q, k, scale):
    return jnp.einsum('bqhd,bkhd->bhqk', q, k).astype(jnp.float32) * scale


def attend(q, k, v, scale):
    p = jax.nn.softmax(scores(q, k, scale), axis=-1).astype(v.dtype)
    return jnp.einsum('bhqk,bkhd->bqhd', p, v)


def diff_attend(q1, q2, k1, k2, v, lam, subln, lambda_init):
    p1 = jax.nn.softmax(scores(q1, k1, DIFF_SCALE), axis=-1)
    p2 = jax.nn.softmax(scores(q2, k2, DIFF_SCALE), axis=-1)
    o = jnp.einsum('bhqk,bkhd->bqhd', (p1 - lam * p2).astype(v.dtype), v)
    return rmsnorm(o, subln) * (1.0 - lambda_init)


def sweep_query_blocks(fn, *qs):
    B, S = qs[0].shape[:2]
    nb = S // Q_BLOCK
    blocks = tuple(jnp.moveaxis(q.reshape(B, nb, Q_BLOCK, *q.shape[2:]), 1, 0) for q in qs)
    out = lax.map(lambda xs: fn(*xs), blocks)
    out = jnp.moveaxis(out, 0, 1)
    return out.reshape(B, S, *out.shape[3:])


def neighbourhood_attention(q, k, v, k_ctx, v_ctx, rpb):
    B, S, H, Dh = q.shape
    rows_n = S // GRID_W
    kh = min(NA_KH, rows_n)
    kw = NA_KW
    r = jnp.arange(rows_n)
    col = jnp.arange(GRID_W)
    r0 = jnp.clip(r - kh // 2, 0, rows_n - kh)
    row_idx = r0[:, None] + jnp.arange(kh)[None, :]
    c0 = jnp.clip(col - kw // 2, 0, GRID_W - kw)
    col_ok = (col[None, :] >= c0[:, None]) & (col[None, :] < c0[:, None] + kw)
    grid = lambda t: t.reshape(B, rows_n, GRID_W, H, Dh)
    qg = grid(q)
    kg = jnp.take(grid(k), row_idx.reshape(-1), axis=1).reshape(B, rows_n, kh, GRID_W, H, Dh)
    vg = jnp.take(grid(v), row_idx.reshape(-1), axis=1).reshape(B, rows_n, kh, GRID_W, H, Dh)
    s_loc = jnp.einsum('brqhd,brjkhd->bhrqjk', qg, kg).astype(jnp.float32) * NA_SCALE
    row_off = row_idx - r[:, None] + (NA_KH - 1)
    col_off = jnp.clip(col[None, :] - col[:, None], -(NA_KW - 1), NA_KW - 1) + (NA_KW - 1)
    bias = rpb[:, row_off[:, None, :, None], col_off[None, :, None, :]]
    s_loc = jnp.where(col_ok[:, None, :], s_loc + bias.astype(jnp.float32), NEG_INF)
    n_loc = kh * GRID_W
    s_loc = s_loc.reshape(B, H, rows_n, GRID_W, n_loc)
    s_ctx = jnp.einsum('brqhd,bkhd->bhrqk', qg, k_ctx).astype(jnp.float32) * NA_SCALE
    p = jax.nn.softmax(jnp.concatenate([s_loc, s_ctx], axis=-1), axis=-1).astype(v.dtype)
    p_loc = p[..., :n_loc].reshape(B, H, rows_n, GRID_W, kh, GRID_W)
    o = (jnp.einsum('bhrqjk,brjkhd->brqhd', p_loc, vg)
         + jnp.einsum('bhrqk,bkhd->brqhd', p[..., n_loc:], v_ctx))
    return o.reshape(B, S, H * Dh)


def token_mixer(h, hc, w_in, na_rpb, mla_q_norm, mla_kv_norm, mla_w_uq, mla_w_ukv,
                diff_lambda, diff_subln, w_branch, w_out, lambda_init, rows, cols, with_ctx_out):
    p = split_cols(h @ w_in, IN_SPLITS)
    pc = split_cols(hc @ w_in, IN_SPLITS)

    na_q, na_k, na_v = (to_heads(t, N_HEADS_NA) for t in p[0:3])
    na_qc, na_kc, na_vc = (to_heads(t, N_HEADS_NA) for t in pc[0:3])
    o_na = neighbourhood_attention(na_q, na_k, na_v, na_kc, na_vc, na_rpb)

    def mla_queries(cq):
        return to_heads(rmsnorm(cq, mla_q_norm) @ mla_w_uq, N_HEADS_MLA)

    def mla_keys_values(ckv, k_pe):
        kv = to_heads(rmsnorm(ckv, mla_kv_norm) @ mla_w_ukv, N_HEADS_MLA)
        k_nope, v = kv[..., :MLA_NOPE_DIM], kv[..., MLA_NOPE_DIM:]
        k_pe = jnp.broadcast_to(k_pe, k_nope.shape[:-1] + (MLA_ROPE_DIM,))
        return jnp.concatenate([k_nope, k_pe], axis=-1), v

    mq = mla_queries(p[3])
    mq = jnp.concatenate([mq[..., :MLA_NOPE_DIM], rope_2d(mq[..., MLA_NOPE_DIM:], rows, cols)], axis=-1)
    mk, mv = mla_keys_values(p[4], rope_2d(p[5][:, :, None, :], rows, cols))
    mkc, mvc = mla_keys_values(pc[4], pc[5][:, :, None, :])
    mk_all = jnp.concatenate([mkc, mk], axis=1)
    mv_all = jnp.concatenate([mvc, mv], axis=1)
    o_mla = sweep_query_blocks(lambda qb: attend(qb, mk_all, mv_all, MLA_SCALE), mq)

    lam = (jnp.exp(jnp.sum(diff_lambda[0] * diff_lambda[1]))
           - jnp.exp(jnp.sum(diff_lambda[2] * diff_lambda[3])) + lambda_init)
    dq, dk, dv = (to_heads(t, N_HEADS_DIFF) for t in p[6:9])
    dqc, dkc, dvc = (to_heads(t, N_HEADS_DIFF) for t in pc[6:9])
    halves = lambda t: (t[..., :DIFF_QK_DIM], t[..., DIFF_QK_DIM:])
    rot = lambda t: rope_2d(t, rows, cols)
    dq1, dq2 = (rot(t) for t in halves(dq))
    dk1, dk2 = (rot(t) for t in halves(dk))
    dk1c, dk2c = halves(dkc)
    dk1_all = jnp.concatenate([dk1c, dk1], axis=1)
    dk2_all = jnp.concatenate([dk2c, dk2], axis=1)
    dv_all = jnp.concatenate([dvc, dv], axis=1)
    o_diff = sweep_query_blocks(
        lambda q1b, q2b: diff_attend(q1b, q2b, dk1_all, dk2_all, dv_all, lam, diff_subln, lambda_init),
        dq1, dq2)

    def merge(a, b, d, g_a, g_b, g_d):
        flat = lambda t: t.reshape(*t.shape[:2], -1)
        y = (jax.nn.sigmoid(g_a) * (flat(a) @ w_branch[:NA_W])
             + jax.nn.sigmoid(g_b) * (flat(b) @ w_branch[NA_W:NA_W + MLA_W])
             + jax.nn.sigmoid(g_d) * (flat(d) @ w_branch[NA_W + MLA_W:]))
        return y @ w_out

    out = merge(o_na, o_mla, o_diff, p[9], p[10], p[11])
    if not with_ctx_out:
        return out, None
    dq1c, dq2c = halves(dqc)
    oc = merge(attend(na_qc, na_kc, na_vc, NA_SCALE),
               attend(mla_queries(pc[3]), mkc, mvc, MLA_SCALE),
               diff_attend(dq1c, dq2c, dk1c, dk2c, dvc, lam, diff_subln, lambda_init),
               pc[9], pc[10], pc[11])
    return out, oc


def setup_inputs(seed: int = 0) -> dict:
    key = jax.random.key(seed)
    ks = jax.random.split(key, 20)
    nrm = lambda k, shape: jax.random.normal(k, shape, jnp.float32)
    w = lambda k, shape, fan_in, gain=1.0: nrm(k, shape) * (gain * fan_in ** -0.5)
    gn = lambda k, shape: 1.0 + 0.1 * nrm(k, shape)
    return {
        "x": nrm(ks[0], (BATCH, SEQ, D_MODEL)),
        "c": nrm(ks[1], (BATCH, D_MODEL)),
        "ctx": nrm(ks[2], (BATCH, CTX_LEN, D_MODEL)),
        "c_ctx": nrm(ks[3], (D_MODEL,)),
        "w_ada": w(ks[4], (DEPTH, D_MODEL, N_MOD * D_MODEL), D_MODEL, 0.5),
        "b_ada": 0.02 * nrm(ks[5], (DEPTH, N_MOD * D_MODEL)),
        "norm_w": gn(ks[6], (DEPTH, 3, D_MODEL)),
        "ffn_w_in": w(ks[7], (DEPTH, 2, D_MODEL, 2 * FFN_DIM), D_MODEL),
        "ffn_w_out": w(ks[8], (DEPTH, 2, FFN_DIM, D_MODEL), FFN_DIM),
        "w_in": w(ks[9], (DEPTH, D_MODEL, IN_W), D_MODEL),
        "na_rpb": 0.1 * nrm(ks[10], (DEPTH, N_HEADS_NA, 2 * NA_KH - 1, 2 * NA_KW - 1)),
        "mla_q_norm": gn(ks[11], (DEPTH, MLA_Q_RANK)),
        "mla_kv_norm": gn(ks[12], (DEPTH, MLA_KV_RANK)),
        "mla_w_uq": w(ks[13], (DEPTH, MLA_Q_RANK, N_HEADS_MLA * (MLA_NOPE_DIM + MLA_ROPE_DIM)), MLA_Q_RANK),
        "mla_w_ukv": w(ks[14], (DEPTH, MLA_KV_RANK, N_HEADS_MLA * (MLA_NOPE_DIM + MLA_V_DIM)), MLA_KV_RANK),
        "diff_lambda": 0.1 * nrm(ks[15], (DEPTH, 4, DIFF_QK_DIM)),
        "diff_subln": gn(ks[16], (DEPTH, HEAD_DIM)),
        "w_branch": w(ks[17], (DEPTH, MIX_W, D_MODEL), MIX_W),
        "w_out": w(ks[18], (DEPTH, D_MODEL, D_MODEL), D_MODEL),
        "final_norm": gn(ks[19], (D_MODEL,)),
    }


def reference(x, c, ctx, c_ctx, w_ada, b_ada, norm_w, ffn_w_in, ffn_w_out, w_in, na_rpb,
              mla_q_norm, mla_kv_norm, mla_w_uq, mla_w_ukv, diff_lambda, diff_subln,
              w_branch, w_out, final_norm):
    S = x.shape[1]
    t = jnp.arange(S)
    rows, cols = t // GRID_W, t % GRID_W
    xc = ctx
    for l in range(DEPTH):
        last = l == DEPTH - 1
        lambda_init = 0.8 - 0.6 * math.exp(-0.3 * l)
        m = [mm[:, None, :] for mm in jnp.split(jax.nn.silu(c) @ w_ada[l] + b_ada[l], N_MOD, axis=-1)]
        mc = jnp.split(jax.nn.silu(c_ctx) @ w_ada[l] + b_ada[l], N_MOD, axis=-1)
        x = x + 0.5 * m[2] * swiglu(modulate(x, norm_w[l, 0], m[0], m[1]), ffn_w_in[l, 0], ffn_w_out[l, 0])
        xc = xc + 0.5 * mc[2] * swiglu(modulate(xc, norm_w[l, 0], mc[0], mc[1]), ffn_w_in[l, 0], ffn_w_out[l, 0])
        o, oc = token_mixer(modulate(x, norm_w[l, 1], m[3], m[4]), modulate(xc, norm_w[l, 1], mc[3], mc[4]),
                            w_in[l], na_rpb[l], mla_q_norm[l], mla_kv_norm[l], mla_w_uq[l], mla_w_ukv[l],
                            diff_lambda[l], diff_subln[l], w_branch[l], w_out[l], lambda_init, rows, cols,
                            not last)
        x = x + m[5] * o
        x = x + 0.5 * m[8] * swiglu(modulate(x, norm_w[l, 2], m[6], m[7]), ffn_w_in[l, 1], ffn_w_out[l, 1])
        if not last:
            xc = xc + mc[5] * oc
            xc = xc + 0.5 * mc[8] * swiglu(modulate(xc, norm_w[l, 2], mc[6], mc[7]), ffn_w_in[l, 1], ffn_w_out[l, 1])
    return rmsnorm(x, final_norm)
```

```python
import functools
import math

import jax
import jax.numpy as jnp
from jax import lax
from jax.experimental import pallas as pl
from jax.experimental.pallas import tpu as pltpu

GRID_W = 64
HEAD_DIM = 128
N_HEADS_NA = 6
N_HEADS_MLA = 5
N_HEADS_DIFF = 5
NA_KH = 8
NA_KW = 16
MLA_Q_RANK = 768
MLA_KV_RANK = 512
MLA_NOPE_DIM = 128
MLA_ROPE_DIM = 64
MLA_V_DIM = 128
DIFF_QK_DIM = HEAD_DIM // 2
N_MOD = 9
ROPE_THETA = 10000.0
NORM_EPS = 1e-6
NEG_INF = -1e30
NA_W = N_HEADS_NA * HEAD_DIM
MLA_W = N_HEADS_MLA * MLA_V_DIM
DIFF_W = N_HEADS_DIFF * HEAD_DIM
NA_SCALE = HEAD_DIM ** -0.5
MLA_SCALE = (MLA_NOPE_DIM + MLA_ROPE_DIM) ** -0.5
DIFF_SCALE = DIFF_QK_DIM ** -0.5

LANES = 128
VMEM_LIMIT = 60 * 1024 * 1024
BF16 = jnp.bfloat16
F32 = jnp.float32


def _params(*sem):
    return pltpu.CompilerParams(dimension_semantics=sem, vmem_limit_bytes=VMEM_LIMIT)


def _pick(n, prefs):
    for p in prefs:
        if n % p == 0:
            return p
    return n


def _dot(a, b):
    return jnp.dot(a, b, preferred_element_type=F32)


def _dot_nt(a, b):
    return lax.dot_general(a, b, (((1,), (1,)), ((), ())), preferred_element_type=F32)


def _rms(x, g):
    return x * lax.rsqrt(jnp.mean(x * x, axis=-1, keepdims=True) + NORM_EPS) * g


def _silu(x):
    return x * (1.0 / (1.0 + jnp.exp(-x)))


def _sigmoid(x):
    return 1.0 / (1.0 + jnp.exp(-x))


def _swap16(x):
    lane = lax.broadcasted_iota(jnp.int32, x.shape, x.ndim - 1)
    down = pltpu.roll(x, 16, x.ndim - 1)
    up = pltpu.roll(x, LANES - 16, x.ndim - 1)
    return jnp.where((lane & 16) != 0, down, up)


def _rope(x, cos, sin):
    return x * cos + _swap16(x) * sin


def _ada_kernel(c_ref, w_ref, b_ref, o_ref):
    a = _silu(c_ref[...]).astype(BF16)
    o_ref[...] = _dot(a, w_ref[...].astype(BF16)) + b_ref[...]


def _ada(cvec, w_ada, b_ada):
    depth, d, n = w_ada.shape
    tn = _pick(n, (1024, 512, 256, 128))
    return pl.pallas_call(
        _ada_kernel,
        grid=(depth, n // tn),
        in_specs=[
            pl.BlockSpec((8, d), lambda l, j: (0, 0)),
            pl.BlockSpec((None, d, tn), lambda l, j: (l, 0, j)),
            pl.BlockSpec((None, 1, tn), lambda l, j: (l, 0, j)),
        ],
        out_specs=pl.BlockSpec((None, 8, tn), lambda l, j: (l, 0, j)),
        out_shape=jax.ShapeDtypeStruct((depth, 8, n), F32),
        compiler_params=_params("parallel", "parallel"),
        name="ada",
    )(cvec, w_ada, b_ada.reshape(depth, 1, n))


def _ffn_kernel(x_ref, mod_ref, nw_ref, wa_ref, wb_ref, wo_ref, fn_ref, o_ref, h_ref, *, mod0, final):
    j = pl.program_id(1)
    nj = pl.num_programs(1)

    @pl.when(j == 0)
    def _():
        x = x_ref[...]
        shift = mod_ref[mod0:mod0 + 1, :]
        scale = mod_ref[mod0 + 1:mod0 + 2, :]
        h_ref[...] = (_rms(x, nw_ref[...]) * (1.0 + scale) + shift).astype(BF16)

    h = h_ref[...]
    a = _dot(h, wa_ref[...].astype(BF16))
    b = _dot(h, wb_ref[...].astype(BF16))
    g = (_silu(a) * b).astype(BF16)
    contrib = _dot(g, wo_ref[...].astype(BF16))

    @pl.when(j == 0)
    def _():
        o_ref[...] = contrib

    @pl.when(j > 0)
    def _():
        o_ref[...] += contrib

    @pl.when(j == nj - 1)
    def _():
        gate = 0.5 * mod_ref[mod0 + 2:mod0 + 3, :]
        y = x_ref[...] + gate * o_ref[...]
        if final:
            y = _rms(y, fn_ref[...])
        o_ref[...] = y


def _ffn(x, mod, nw, w_in, w_out, fn, *, l, s, mod0, rows_per_group, tm, final=False):
    n, d = x.shape
    f = w_out.shape[2]
    fc = _pick(f, (256, 128))
    nf = f // fc
    tiles_per_group = rows_per_group // tm
    return pl.pallas_call(
        functools.partial(_ffn_kernel, mod0=mod0, final=final),
        grid=(n // tm, nf),
        in_specs=[
            pl.BlockSpec((tm, d), lambda i, j: (i, 0), pipeline_mode=pl.Buffered(1)),
            pl.BlockSpec((None, N_MOD, d), lambda i, j: (i // tiles_per_group, 0, 0)),
            pl.BlockSpec((1, d), lambda i, j: (0, 0)),
            pl.BlockSpec((None, None, d, fc), lambda i, j: (l, s, 0, j)),
            pl.BlockSpec((None, None, d, fc), lambda i, j: (l, s, 0, nf + j)),
            pl.BlockSpec((None, None, fc, d), lambda i, j: (l, s, j, 0)),
            pl.BlockSpec((1, d), lambda i, j: (0, 0)),
        ],
        out_specs=pl.BlockSpec((tm, d), lambda i, j: (i, 0)),
        out_shape=jax.ShapeDtypeStruct((n, d), F32),
        scratch_shapes=[pltpu.VMEM((tm, d), BF16)],
        compiler_params=_params("parallel", "arbitrary"),
        name="ffn",
    )(x, mod, nw, w_in, w_in, w_out, fn)


def _inproj_kernel(x_ref, mod_ref, nw_ref, w_ref, o_ref, h_ref):
    @pl.when(pl.program_id(1) == 0)
    def _():
        shift = mod_ref[3:4, :]
        scale = mod_ref[4:5, :]
        h_ref[...] = (_rms(x_ref[...], nw_ref[...]) * (1.0 + scale) + shift).astype(BF16)

    o_ref[...] = _dot(h_ref[...], w_ref[...]).astype(BF16)


def _inproj(x, mod, nw, w, *, rows_per_group, tm):
    n, d = x.shape
    nw_cols = w.shape[1]
    tn = _pick(nw_cols, (512, 256, 128))
    tiles_per_group = rows_per_group // tm
    return pl.pallas_call(
        _inproj_kernel,
        grid=(n // tm, nw_cols // tn),
        in_specs=[
            pl.BlockSpec((tm, d), lambda i, j: (i, 0)),
            pl.BlockSpec((None, N_MOD, d), lambda i, j: (i // tiles_per_group, 0, 0)),
            pl.BlockSpec((1, d), lambda i, j: (0, 0)),
            pl.BlockSpec((d, tn), lambda i, j: (0, j)),
        ],
        out_specs=pl.BlockSpec((tm, tn), lambda i, j: (i, j)),
        out_shape=jax.ShapeDtypeStruct((n, nw_cols), BF16),
        scratch_shapes=[pltpu.VMEM((tm, d), BF16)],
        compiler_params=_params("parallel", "arbitrary"),
        name="inproj",
    )(x, mod, nw, w)


class _Cols:
    def __init__(self, d):
        self.na_q = 0
        self.na_k = NA_W
        self.na_v = 2 * NA_W
        self.cq = 3 * NA_W
        self.ckv = self.cq + MLA_Q_RANK
        self.gates = self.ckv + MLA_KV_RANK
        self.dq = self.gates + 3 * d
        self.dk = self.dq + DIFF_W
        self.dv = self.dk + DIFF_W
        self.kpe = self.dv + DIFF_W
        self.total = self.kpe + LANES


def _pack_w_in(w, d):
    o = 3 * NA_W + MLA_Q_RANK + MLA_KV_RANK
    kpe = w[:, o:o + MLA_ROPE_DIM]
    o2 = o + MLA_ROPE_DIM
    diff = w[:, o2:o2 + 3 * DIFF_W]
    gates = w[:, o2 + 3 * DIFF_W:]
    pad = jnp.zeros((w.shape[0], LANES - MLA_ROPE_DIM), w.dtype)
    return jnp.concatenate([w[:, :o], gates, diff, kpe, pad], axis=1).astype(BF16)


def _softmax_pv(parts):
    mx = parts[0][0].max(axis=-1, keepdims=True)
    for s, _ in parts[1:]:
        mx = jnp.maximum(mx, s.max(axis=-1, keepdims=True))
    acc = None
    den = None
    for s, v in parts:
        e = jnp.exp(s - mx)
        den_i = e.sum(axis=-1, keepdims=True)
        o_i = _dot(e.astype(BF16), v)
        acc = o_i if acc is None else acc + o_i
        den = den_i if den is None else den + den_i
    return acc * (1.0 / den)


def _na_kernel(q_ref, k_ref, v_ref, kc_ref, vc_ref, bias_ref, o_ref, *, rows_per_step, n_rows):
    i = pl.program_id(1)
    kh = min(NA_KH, n_rows)
    n_loc = kh * GRID_W

    def row_body(rr, carry):
        r = i * rows_per_step + rr
        r0 = jnp.clip(r - kh // 2, 0, n_rows - kh)
        dsel = r0 - r + (NA_KH - 1)
        q0 = pl.multiple_of(rr * GRID_W, GRID_W)
        k0 = pl.multiple_of(r0 * GRID_W, GRID_W)
        for h in range(N_HEADS_NA):
            cs = slice(h * HEAD_DIM, (h + 1) * HEAD_DIM)
            q = q_ref[pl.ds(q0, GRID_W), cs]
            k = k_ref[pl.ds(k0, n_loc), cs]
            v = v_ref[pl.ds(k0, n_loc), cs]
            s_loc = _dot_nt(q, k) * NA_SCALE + bias_ref[dsel, h]
            s_ctx = _dot_nt(q, kc_ref[:, cs]) * NA_SCALE
            o = _softmax_pv([(s_loc, v), (s_ctx, vc_ref[:, cs])])
            o_ref[pl.ds(q0, GRID_W), cs] = o.astype(BF16)
        return carry

    lax.fori_loop(0, rows_per_step, row_body, 0)


def _na_bias(rpb, n_rows):
    kh = min(NA_KH, n_rows)
    col = jnp.arange(GRID_W)
    c0 = jnp.clip(col - NA_KW // 2, 0, GRID_W - NA_KW)
    col_ok = (col[None, :] >= c0[:, None]) & (col[None, :] < c0[:, None] + NA_KW)
    col_off = jnp.clip(col[None, :] - col[:, None], -(NA_KW - 1), NA_KW - 1) + (NA_KW - 1)
    d = jnp.arange(NA_KH)
    row_off = jnp.clip(d[:, None] + jnp.arange(kh)[None, :], 0, 2 * NA_KH - 2)
    b = rpb[:, row_off[:, None, :, None], col_off[None, :, None, :]]
    b = jnp.where(col_ok[None, None, :, None, :], b.astype(F32), NEG_INF)
    return jnp.moveaxis(b, 0, 1).reshape(NA_KH, N_HEADS_NA, GRID_W, kh * GRID_W)


def _na_attn(px, pc, bias, cols, *, batch, seq, ctx):
    n_rows = seq // GRID_W
    rows_per_step = _pick(n_rows, (8, 4, 2, 1))
    steps = n_rows // rows_per_step
    tq = rows_per_step * GRID_W
    qb, kb, vb = cols.na_q // NA_W, cols.na_k // NA_W, cols.na_v // NA_W
    return pl.pallas_call(
        functools.partial(_na_kernel, rows_per_step=rows_per_step, n_rows=n_rows),
        grid=(batch, steps),
        in_specs=[
            pl.BlockSpec((tq, NA_W), lambda b, i: (b * steps + i, qb)),
            pl.BlockSpec((seq, NA_W), lambda b, i: (b, kb)),
            pl.BlockSpec((seq, NA_W), lambda b, i: (b, vb)),
            pl.BlockSpec((ctx, NA_W), lambda b, i: (b, kb)),
            pl.BlockSpec((ctx, NA_W), lambda b, i: (b, vb)),
            pl.BlockSpec(bias.shape, lambda b, i: (0, 0, 0, 0)),
        ],
        out_specs=pl.BlockSpec((tq, NA_W), lambda b, i: (b * steps + i, 0)),
        out_shape=jax.ShapeDtypeStruct((batch * seq, NA_W), BF16),
        compiler_params=_params("parallel", "arbitrary"),
        name="na_attn",
    )(px, px, px, pc, pc, bias)


def _ctx_na_kernel(q_ref, k_ref, v_ref, o_ref):
    for h in range(N_HEADS_NA):
        cs = slice(h * HEAD_DIM, (h + 1) * HEAD_DIM)
        s = _dot_nt(q_ref[:, cs], k_ref[:, cs]) * NA_SCALE
        o_ref[:, cs] = _softmax_pv([(s, v_ref[:, cs])]).astype(BF16)


def _ctx_na_attn(pc, cols, *, batch, ctx):
    qb, kb, vb = cols.na_q // NA_W, cols.na_k // NA_W, cols.na_v // NA_W
    return pl.pallas_call(
        _ctx_na_kernel,
        grid=(batch,),
        in_specs=[
            pl.BlockSpec((ctx, NA_W), lambda b: (b, qb)),
            pl.BlockSpec((ctx, NA_W), lambda b: (b, kb)),
            pl.BlockSpec((ctx, NA_W), lambda b: (b, vb)),
        ],
        out_specs=pl.BlockSpec((ctx, NA_W), lambda b: (b, 0)),
        out_shape=jax.ShapeDtypeStruct((batch * ctx, NA_W), BF16),
        compiler_params=_params("parallel"),
        name="ctx_na_attn",
    )(pc, pc, pc)


def _mla_kernel(*refs, ctx, seq, latent):
    if latent:
        (cq_ref, ckvc_ref, kpec_ref, ckvl_ref, kpel_ref, cosq_ref, sinq_ref, cosk_ref, sink_ref,
         qn_ref, kvn_ref, wuq_ref, wukv_ref, o_ref, k_scr, v_scr) = refs
    else:
        cq_ref, ckvc_ref, kpec_ref, qn_ref, kvn_ref, wuq_ref, wukv_ref, o_ref, k_scr, v_scr = refs

    @pl.when(pl.program_id(2) == 0)
    def _():
        w = wukv_ref[...]

        def fill(row0, n, ckv_ref, kpe):
            kv = _dot(_rms(ckv_ref[...].astype(F32), kvn_ref[...]).astype(BF16), w)
            k_scr[row0:row0 + n, 0:MLA_NOPE_DIM] = kv[:, :MLA_NOPE_DIM].astype(BF16)
            k_scr[row0:row0 + n, MLA_NOPE_DIM:] = kpe.astype(BF16)
            v_scr[row0:row0 + n, :] = kv[:, MLA_NOPE_DIM:].astype(BF16)

        fill(0, ctx, ckvc_ref, kpec_ref[...])
        if latent:
            fill(ctx, seq, ckvl_ref, _rope(kpel_ref[...].astype(F32), cosk_ref[...], sink_ref[...]))

    q = _dot(_rms(cq_ref[...].astype(F32), qn_ref[...]).astype(BF16), wuq_ref[...])
    q_pe = q[:, MLA_NOPE_DIM:]
    if latent:
        q_pe = _rope(q_pe, cosq_ref[...], sinq_ref[...])
    q = jnp.concatenate([q[:, :MLA_NOPE_DIM], q_pe], axis=1).astype(BF16)
    s = _dot_nt(q, k_scr[...]) * MLA_SCALE
    o_ref[...] = _softmax_pv([(s, v_scr[...])]).astype(BF16)


def _mla_attn(pq, pc, px, cos, sin, qn, kvn, wuq, wukv, cols, *, batch, n_q, ctx, seq, latent):
    tq = _pick(n_q, (256, 128))
    nq = n_q // tq
    cqb, ckvb, kpeb = cols.cq // MLA_Q_RANK, cols.ckv // MLA_KV_RANK, cols.kpe // LANES
    n_keys = ctx + (seq if latent else 0)
    in_specs = [
        pl.BlockSpec((tq, MLA_Q_RANK), lambda b, h, i: (b * nq + i, cqb)),
        pl.BlockSpec((ctx, MLA_KV_RANK), lambda b, h, i: (b, ckvb)),
        pl.BlockSpec((ctx, LANES), lambda b, h, i: (b, kpeb)),
    ]
    args = [pq, pc, pc]
    if latent:
        in_specs += [
            pl.BlockSpec((seq, MLA_KV_RANK), lambda b, h, i: (b, ckvb)),
            pl.BlockSpec((seq, LANES), lambda b, h, i: (b, kpeb)),
            pl.BlockSpec((tq, LANES), lambda b, h, i: (i, 0)),
            pl.BlockSpec((tq, LANES), lambda b, h, i: (i, 0)),
            pl.BlockSpec((seq, LANES), lambda b, h, i: (0, 0)),
            pl.BlockSpec((seq, LANES), lambda b, h, i: (0, 0)),
        ]
        args += [px, px, cos, sin, cos, sin]
    in_specs += [
        pl.BlockSpec((1, MLA_Q_RANK), lambda b, h, i: (0, 0)),
        pl.BlockSpec((1, MLA_KV_RANK), lambda b, h, i: (0, 0)),
        pl.BlockSpec((None, MLA_Q_RANK, 2 * LANES), lambda b, h, i: (h, 0, 0)),
        pl.BlockSpec((None, MLA_KV_RANK, 2 * LANES), lambda b, h, i: (h, 0, 0)),
    ]
    args += [qn, kvn, wuq, wukv]
    return pl.pallas_call(
        functools.partial(_mla_kernel, ctx=ctx, seq=seq, latent=latent),
        grid=(batch, N_HEADS_MLA, nq),
        in_specs=in_specs,
        out_specs=pl.BlockSpec((tq, MLA_V_DIM), lambda b, h, i: (b * nq + i, h)),
        out_shape=jax.ShapeDtypeStruct((batch * n_q, MLA_W), BF16),
        scratch_shapes=[pltpu.VMEM((n_keys, 2 * LANES), BF16), pltpu.VMEM((n_keys, MLA_V_DIM), BF16)],
        compiler_params=_params("parallel", "parallel", "arbitrary"),
        name="mla_attn" if latent else "ctx_mla_attn",
    )(*args)


def _diff_kernel(*refs, ctx, seq, latent, lambda_init):
    if latent:
        (q_ref, kc_ref, vc_ref, kl_ref, vl_ref, cosq_ref, sinq_ref, cosk_ref, sink_ref,
         lam_ref, sub_ref, o_ref, k_scr, v_scr) = refs
    else:
        q_ref, kc_ref, vc_ref, lam_ref, sub_ref, o_ref, k_scr, v_scr = refs

    @pl.when(pl.program_id(2) == 0)
    def _():
        k_scr[0:ctx, :] = kc_ref[...]
        v_scr[0:ctx, :] = vc_ref[...]
        if latent:
            k_scr[ctx:ctx + seq, :] = _rope(kl_ref[...].astype(F32), cosk_ref[...], sink_ref[...]).astype(BF16)
            v_scr[ctx:ctx + seq, :] = vl_ref[...]

    dl = lam_ref[...]
    lam = (jnp.exp(jnp.sum(dl[0:1] * dl[1:2], axis=-1, keepdims=True))
           - jnp.exp(jnp.sum(dl[2:3] * dl[3:4], axis=-1, keepdims=True)) + lambda_init)

    q = q_ref[...].astype(F32)
    if latent:
        q = _rope(q, cosq_ref[...], sinq_ref[...])
    lane = lax.broadcasted_iota(jnp.int32, q.shape, 1)
    q1 = jnp.where(lane < DIFF_QK_DIM, q, 0.0).astype(BF16)
    q2 = jnp.where(lane >= DIFF_QK_DIM, q, 0.0).astype(BF16)
    k = k_scr[...]

    def probs(qm):
        s = _dot_nt(qm, k) * DIFF_SCALE
        e = jnp.exp(s - s.max(axis=-1, keepdims=True))
        return e, 1.0 / e.sum(axis=-1, keepdims=True)

    e1, r1 = probs(q1)
    e2, r2 = probs(q2)
    w = (e1 * r1 - e2 * (lam * r2)).astype(BF16)
    o = _dot(w, v_scr[...])
    o_ref[...] = (_rms(o, sub_ref[...]) * (1.0 - lambda_init)).astype(BF16)


def _diff_attn(pq, pc, px, cos, sin, dlam, subln, cols, *, batch, n_q, ctx, seq, latent, lambda_init):
    tq = _pick(n_q, (256, 128))
    nq = n_q // tq
    qb, kb, vb = cols.dq // HEAD_DIM, cols.dk // HEAD_DIM, cols.dv // HEAD_DIM
    n_keys = ctx + (seq if latent else 0)
    in_specs = [
        pl.BlockSpec((tq, HEAD_DIM), lambda b, h, i: (b * nq + i, qb + h)),
        pl.BlockSpec((ctx, HEAD_DIM), lambda b, h, i: (b, kb + h)),
        pl.BlockSpec((ctx, HEAD_DIM), lambda b, h, i: (b, vb + h)),
    ]
    args = [pq, pc, pc]
    if latent:
        in_specs += [
            pl.BlockSpec((seq, HEAD_DIM), lambda b, h, i: (b, kb + h)),
            pl.BlockSpec((seq, HEAD_DIM), lambda b, h, i: (b, vb + h)),
            pl.BlockSpec((tq, LANES), lambda b, h, i: (i, 0)),
            pl.BlockSpec((tq, LANES), lambda b, h, i: (i, 0)),
            pl.BlockSpec((seq, LANES), lambda b, h, i: (0, 0)),
            pl.BlockSpec((seq, LANES), lambda b, h, i: (0, 0)),
        ]
        args += [px, px, cos, sin, cos, sin]
    in_specs += [
        pl.BlockSpec((4, DIFF_QK_DIM), lambda b, h, i: (0, 0)),
        pl.BlockSpec((1, HEAD_DIM), lambda b, h, i: (0, 0)),
    ]
    args += [dlam, subln]
    return pl.pallas_call(
        functools.partial(_diff_kernel, ctx=ctx, seq=seq, latent=latent, lambda_init=lambda_init),
        grid=(batch, N_HEADS_DIFF, nq),
        in_specs=in_specs,
        out_specs=pl.BlockSpec((tq, HEAD_DIM), lambda b, h, i: (b * nq + i, h)),
        out_shape=jax.ShapeDtypeStruct((batch * n_q, DIFF_W), BF16),
        scratch_shapes=[pltpu.VMEM((n_keys, HEAD_DIM), BF16), pltpu.VMEM((n_keys, HEAD_DIM), BF16)],
        compiler_params=_params("parallel", "parallel", "arbitrary"),
        name="diff_attn" if latent else "ctx_diff_attn",
    )(*args)


def _merge_kernel(oa_ref, ob_ref, od_ref, ga_ref, gb_ref, gd_ref, wa_ref, wb_ref, wd_ref, y_ref):
    y = (_sigmoid(ga_ref[...].astype(F32)) * _dot(oa_ref[...], wa_ref[...])
         + _sigmoid(gb_ref[...].astype(F32)) * _dot(ob_ref[...], wb_ref[...])
         + _sigmoid(gd_ref[...].astype(F32)) * _dot(od_ref[...], wd_ref[...]))
    y_ref[...] = y.astype(BF16)


def _merge(o_na, o_mla, o_diff, p, wb_na, wb_mla, wb_diff, cols, *, tm):
    n = o_na.shape[0]
    d = wb_na.shape[1]
    tn = _pick(d, (512, 256, 128))
    g0 = cols.gates // tn
    gs = d // tn
    return pl.pallas_call(
        _merge_kernel,
        grid=(n // tm, d // tn),
        in_specs=[
            pl.BlockSpec((tm, NA_W), lambda i, j: (i, 0)),
            pl.BlockSpec((tm, MLA_W), lambda i, j: (i, 0)),
            pl.BlockSpec((tm, DIFF_W), lambda i, j: (i, 0)),
            pl.BlockSpec((tm, tn), lambda i, j: (i, g0 + j)),
            pl.BlockSpec((tm, tn), lambda i, j: (i, g0 + gs + j)),
            pl.BlockSpec((tm, tn), lambda i, j: (i, g0 + 2 * gs + j)),
            pl.BlockSpec((NA_W, tn), lambda i, j: (0, j)),
            pl.BlockSpec((MLA_W, tn), lambda i, j: (0, j)),
            pl.BlockSpec((DIFF_W, tn), lambda i, j: (0, j)),
        ],
        out_specs=pl.BlockSpec((tm, tn), lambda i, j: (i, j)),
        out_shape=jax.ShapeDtypeStruct((n, d), BF16),
        compiler_params=_params("parallel", "parallel"),
        name="merge",
    )(o_na, o_mla, o_diff, p, p, p, wb_na, wb_mla, wb_diff)


def _outproj_kernel(x_ref, y_ref, mod_ref, w_ref, o_ref):
    o_ref[...] = x_ref[...] + mod_ref[...] * _dot(y_ref[...], w_ref[...].astype(BF16))


def _outproj(x, y, gate, w_out, *, l, rows_per_group, tm):
    n, d = x.shape
    tn = _pick(d, (512, 256, 128))
    tiles_per_group = rows_per_group // tm
    return pl.pallas_call(
        _outproj_kernel,
        grid=(n // tm, d // tn),
        in_specs=[
            pl.BlockSpec((tm, tn), lambda i, j: (i, j)),
            pl.BlockSpec((tm, d), lambda i, j: (i, 0)),
            pl.BlockSpec((None, 1, tn), lambda i, j: (i // tiles_per_group, 0, j)),
            pl.BlockSpec((None, d, tn), lambda i, j: (l, 0, j)),
        ],
        out_specs=pl.BlockSpec((tm, tn), lambda i, j: (i, j)),
        out_shape=jax.ShapeDtypeStruct((n, d), F32),
        compiler_params=_params("parallel", "parallel"),
        name="outproj",
    )(x, y, gate, w_out)


def _rope_tables(seq):
    quarter = MLA_ROPE_DIM // 4
    freqs = ROPE_THETA ** (-jnp.arange(quarter, dtype=F32) / quarter)
    t = jnp.arange(seq)
    rows, cols = t // GRID_W, t % GRID_W

    def unit(pos):
        ang = pos.astype(F32)[:, None] * freqs
        c, s = jnp.cos(ang), jnp.sin(ang)
        return jnp.concatenate([c, c], axis=1), jnp.concatenate([-s, s], axis=1)

    cr, sr = unit(rows)
    cc, sc = unit(cols)
    cos = jnp.concatenate([cr, cc], axis=1)
    sin = jnp.concatenate([sr, sc], axis=1)
    reps = LANES // MLA_ROPE_DIM
    return jnp.tile(cos, (1, reps)), jnp.tile(sin, (1, reps))


def _pack_heads(w, n_heads, width):
    k = w.shape[0]
    per = w.shape[1] // n_heads
    w = jnp.moveaxis(w.reshape(k, n_heads, per), 1, 0)
    if per < width:
        w = jnp.concatenate([w, jnp.zeros((n_heads, k, width - per), w.dtype)], axis=2)
    return w.astype(BF16)


def kernel(x, c, ctx, c_ctx, w_ada, b_ada, norm_w, ffn_w_in, ffn_w_out, w_in, na_rpb, mla_q_norm, mla_kv_norm,
           mla_w_uq, mla_w_ukv, diff_lambda, diff_subln, w_branch, w_out, final_norm):
    batch, seq, d = x.shape
    n_ctx = ctx.shape[1]
    depth = w_ada.shape[0]
    cols = _Cols(d)
    tm_x = _pick(seq, (1024, 512, 256, 128))
    tm_c = batch * n_ctx

    xs = x.reshape(batch * seq, d)
    xc = ctx.reshape(batch * n_ctx, d)
    cvec = jnp.concatenate([c, c_ctx[None, :], jnp.zeros((8 - batch - 1, d), F32)], axis=0)
    mods = _ada(cvec, w_ada, b_ada).reshape(depth, 8, N_MOD, d)
    cos, sin = _rope_tables(seq)
    fn = final_norm.reshape(1, d)

    for l in range(depth):
        last = l == depth - 1
        lambda_init = 0.8 - 0.6 * math.exp(-0.3 * l)
        m = mods[l, :batch]
        mc = mods[l, batch:batch + 1]
        nw = norm_w[l].reshape(3, 1, d)

        xs = _ffn(xs, m, nw[0], ffn_w_in, ffn_w_out, fn, l=l, s=0, mod0=0, rows_per_group=seq, tm=tm_x)
        xc = _ffn(xc, mc, nw[0], ffn_w_in, ffn_w_out, fn, l=l, s=0, mod0=0, rows_per_group=tm_c, tm=tm_c)

        wp = _pack_w_in(w_in[l], d)
        px = _inproj(xs, m, nw[1], wp, rows_per_group=seq, tm=tm_x)
        pc = _inproj(xc, mc, nw[1], wp, rows_per_group=tm_c, tm=tm_c)

        bias = _na_bias(na_rpb[l], seq // GRID_W)
        qn = mla_q_norm[l].reshape(1, MLA_Q_RANK)
        kvn = mla_kv_norm[l].reshape(1, MLA_KV_RANK)
        wuq = _pack_heads(mla_w_uq[l], N_HEADS_MLA, 2 * LANES)
        wukv = _pack_heads(mla_w_ukv[l], N_HEADS_MLA, 2 * LANES)
        subln = diff_subln[l].reshape(1, HEAD_DIM)
        wb = w_branch[l].astype(BF16)
        wb_na, wb_mla, wb_diff = wb[:NA_W], wb[NA_W:NA_W + MLA_W], wb[NA_W + MLA_W:]

        o_na = _na_attn(px, pc, bias, cols, batch=batch, seq=seq, ctx=n_ctx)
        o_mla = _mla_attn(px, pc, px, cos, sin, qn, kvn, wuq, wukv, cols,
                          batch=batch, n_q=seq, ctx=n_ctx, seq=seq, latent=True)
        o_diff = _diff_attn(px, pc, px, cos, sin, diff_lambda[l], subln, cols,
                            batch=batch, n_q=seq, ctx=n_ctx, seq=seq, latent=True, lambda_init=lambda_init)
        y = _merge(o_na, o_mla, o_diff, px, wb_na, wb_mla, wb_diff, cols, tm=tm_x)
        xs = _outproj(xs, y, m[:, 5:6], w_out, l=l, rows_per_group=seq, tm=tm_x)
        xs = _ffn(xs, m, nw[2], ffn_w_in, ffn_w_out, fn, l=l, s=1, mod0=6, rows_per_group=seq, tm=tm_x,
                  final=last)

        if not last:
            oc_na = _ctx_na_attn(pc, cols, batch=batch, ctx=n_ctx)
            oc_mla = _mla_attn(pc, pc, px, cos, sin, qn, kvn, wuq, wukv, cols,
                               batch=batch, n_q=n_ctx, ctx=n_ctx, seq=seq, latent=False)
            oc_diff = _diff_attn(pc, pc, px, cos, sin, diff_lambda[l], subln, cols,
                                 batch=batch, n_q=n_ctx, ctx=n_ctx, seq=seq, latent=False, lambda_init=lambda_init)
            yc = _merge(oc_na, oc_mla, oc_diff, pc, wb_na, wb_mla, wb_diff, cols, tm=tm_c)
            xc = _outproj(xc, yc, mc[:, 5:6], w_out, l=l, rows_per_group=tm_c, tm=tm_c)
            xc = _ffn(xc, mc, nw[2], ffn_w_in, ffn_w_out, fn, l=l, s=1, mod0=6, rows_per_group=tm_c, tm=tm_c)

    return xs.reshape(batch, seq, d)
```

```python
import functools
import math

import jax
import jax.numpy as jnp
from jax import lax
from jax.experimental import pallas as pl
from jax.experimental.pallas import tpu as pltpu

GRID_W = 64
HEAD_DIM = 128
N_HEADS_NA = 6
N_HEADS_MLA = 5
N_HEADS_DIFF = 5
NA_KH = 8
NA_KW = 16
MLA_Q_RANK = 768
MLA_KV_RANK = 512
MLA_NOPE_DIM = 128
MLA_ROPE_DIM = 64
MLA_V_DIM = 128
DIFF_QK_DIM = HEAD_DIM // 2
N_MOD = 9
ROPE_THETA = 10000.0
NORM_EPS = 1e-6
NEG_INF = -1e30
NA_W = N_HEADS_NA * HEAD_DIM
MLA_W = N_HEADS_MLA * MLA_V_DIM
DIFF_W = N_HEADS_DIFF * HEAD_DIM
NA_SCALE = HEAD_DIM ** -0.5
MLA_SCALE = (MLA_NOPE_DIM + MLA_ROPE_DIM) ** -0.5
DIFF_SCALE = DIFF_QK_DIM ** -0.5

LANES = 128
VMEM_LIMIT = 60 * 1024 * 1024
BF16 = jnp.bfloat16
F32 = jnp.float32


def _params(*sem):
    return pltpu.CompilerParams(dimension_semantics=sem, vmem_limit_bytes=VMEM_LIMIT)


def _pick(n, prefs):
    for p in prefs:
        if n % p == 0:
            return p
    return n


def _dot(a, b):
    return jnp.dot(a, b, preferred_element_type=F32)


def _dot_nt(a, b):
    return lax.dot_general(a, b, (((1,), (1,)), ((), ())), preferred_element_type=F32)


def _rms(x, g):
    return x * lax.rsqrt(jnp.mean(x * x, axis=-1, keepdims=True) + NORM_EPS) * g


def _silu(x):
    return x * (1.0 / (1.0 + jnp.exp(-x)))


def _sigmoid(x):
    return 1.0 / (1.0 + jnp.exp(-x))


def _swap16(x):
    lane = lax.broadcasted_iota(jnp.int32, x.shape, x.ndim - 1)
    down = pltpu.roll(x, 16, x.ndim - 1)
    up = pltpu.roll(x, LANES - 16, x.ndim - 1)
    return jnp.where((lane & 16) != 0, down, up)


def _rope(x, cos, sin):
    return x * cos + _swap16(x) * sin


def _ada_kernel(c_ref, w_ref, b_ref, o_ref):
    a = _silu(c_ref[...]).astype(BF16)
    o_ref[...] = _dot(a, w_ref[...].astype(BF16)) + b_ref[...]


def _ada(cvec, w_ada, b_ada):
    depth, d, n = w_ada.shape
    tn = _pick(n, (1024, 512, 256, 128))
    return pl.pallas_call(
        _ada_kernel,
        grid=(depth, n // tn),
        in_specs=[
            pl.BlockSpec((8, d), lambda l, j: (0, 0)),
            pl.BlockSpec((None, d, tn), lambda l, j: (l, 0, j)),
            pl.BlockSpec((None, 1, tn), lambda l, j: (l, 0, j)),
        ],
        out_specs=pl.BlockSpec((None, 8, tn), lambda l, j: (l, 0, j)),
        out_shape=jax.ShapeDtypeStruct((depth, 8, n), F32),
        compiler_params=_params("parallel", "parallel"),
        name="ada",
    )(cvec, w_ada, b_ada.reshape(depth, 1, n))


def _ffn_in_kernel(x_ref, mod_ref, nw_ref, wa_ref, wb_ref, g_ref, h_ref, *, mod0):
    @pl.when(pl.program_id(1) == 0)
    def _():
        shift = mod_ref[mod0:mod0 + 1, :]
        scale = mod_ref[mod0 + 1:mod0 + 2, :]
        h_ref[...] = (_rms(x_ref[...], nw_ref[...]) * (1.0 + scale) + shift).astype(BF16)

    h = h_ref[...]
    a = _dot(h, wa_ref[...].astype(BF16))
    b = _dot(h, wb_ref[...].astype(BF16))
    g_ref[...] = (_silu(a) * b).astype(BF16)


def _ffn_out_kernel(x_ref, g_ref, mod_ref, w_ref, o_ref, *, mod0):
    gate = 0.5 * mod_ref[mod0 + 2:mod0 + 3, :]
    o_ref[...] = x_ref[...] + gate * _dot(g_ref[...], w_ref[...].astype(BF16))


def _ffn(x, mod, nw, w_in, w_out, *, l, s, mod0, rows_per_group, tm):
    n, d = x.shape
    f = w_out.shape[2]
    fc = _pick(f, (512, 256, 128))
    nf = f // fc
    tn = _pick(d, (256, 128))
    tiles_per_group = rows_per_group // tm
    g = pl.pallas_call(
        functools.partial(_ffn_in_kernel, mod0=mod0),
        grid=(n // tm, nf),
        in_specs=[
            pl.BlockSpec((tm, d), lambda i, j: (i, 0)),
            pl.BlockSpec((None, N_MOD, d), lambda i, j: (i // tiles_per_group, 0, 0)),
            pl.BlockSpec((1, d), lambda i, j: (0, 0)),
            pl.BlockSpec((None, None, d, fc), lambda i, j: (l, s, 0, j)),
            pl.BlockSpec((None, None, d, fc), lambda i, j: (l, s, 0, nf + j)),
        ],
        out_specs=pl.BlockSpec((tm, fc), lambda i, j: (i, j)),
        out_shape=jax.ShapeDtypeStruct((n, f), BF16),
        scratch_shapes=[pltpu.VMEM((tm, d), BF16)],
        compiler_params=_params("parallel", "arbitrary"),
        name="ffn_in",
    )(x, mod, nw, w_in, w_in)
    return pl.pallas_call(
        functools.partial(_ffn_out_kernel, mod0=mod0),
        grid=(n // tm, d // tn),
        in_specs=[
            pl.BlockSpec((tm, tn), lambda i, j: (i, j)),
            pl.BlockSpec((tm, f), lambda i, j: (i, 0)),
            pl.BlockSpec((None, N_MOD, tn), lambda i, j: (i // tiles_per_group, 0, j)),
            pl.BlockSpec((None, None, f, tn), lambda i, j: (l, s, 0, j)),
        ],
        out_specs=pl.BlockSpec((tm, tn), lambda i, j: (i, j)),
        out_shape=jax.ShapeDtypeStruct((n, d), F32),
        compiler_params=_params("parallel", "parallel"),
        name="ffn_out",
    )(x, g, mod, w_out)


def _final_norm_kernel(x_ref, g_ref, o_ref):
    o_ref[...] = _rms(x_ref[...], g_ref[...])


def _final_norm(x, g, *, tm):
    n, d = x.shape
    return pl.pallas_call(
        _final_norm_kernel,
        grid=(n // tm,),
        in_specs=[pl.BlockSpec((tm, d), lambda i: (i, 0)), pl.BlockSpec((1, d), lambda i: (0, 0))],
        out_specs=pl.BlockSpec((tm, d), lambda i: (i, 0)),
        out_shape=jax.ShapeDtypeStruct((n, d), F32),
        compiler_params=_params("parallel"),
        name="final_norm",
    )(x, g)


def _inproj_kernel(x_ref, mod_ref, nw_ref, w_ref, o_ref, h_ref):
    @pl.when(pl.program_id(1) == 0)
    def _():
        shift = mod_ref[3:4, :]
        scale = mod_ref[4:5, :]
        h_ref[...] = (_rms(x_ref[...], nw_ref[...]) * (1.0 + scale) + shift).astype(BF16)

    o_ref[...] = _dot(h_ref[...], w_ref[...]).astype(BF16)


def _inproj(x, mod, nw, w, *, rows_per_group, tm):
    n, d = x.shape
    nw_cols = w.shape[1]
    tn = _pick(nw_cols, (512, 256, 128))
    tiles_per_group = rows_per_group // tm
    return pl.pallas_call(
        _inproj_kernel,
        grid=(n // tm, nw_cols // tn),
        in_specs=[
            pl.BlockSpec((tm, d), lambda i, j: (i, 0)),
            pl.BlockSpec((None, N_MOD, d), lambda i, j: (i // tiles_per_group, 0, 0)),
            pl.BlockSpec((1, d), lambda i, j: (0, 0)),
            pl.BlockSpec((d, tn), lambda i, j: (0, j)),
        ],
        out_specs=pl.BlockSpec((tm, tn), lambda i, j: (i, j)),
        out_shape=jax.ShapeDtypeStruct((n, nw_cols), BF16),
        scratch_shapes=[pltpu.VMEM((tm, d), BF16)],
        compiler_params=_params("parallel", "arbitrary"),
        name="inproj",
    )(x, mod, nw, w)


class _Cols:
    def __init__(self, d):
        self.na_q = 0
        self.na_k = NA_W
        self.na_v = 2 * NA_W
        self.cq = 3 * NA_W
        self.ckv = self.cq + MLA_Q_RANK
        self.gates = self.ckv + MLA_KV_RANK
        self.dq = self.gates + 3 * d
        self.dk = self.dq + DIFF_W
        self.dv = self.dk + DIFF_W
        self.kpe = self.dv + DIFF_W
        self.total = self.kpe + LANES


def _pack_w_in(w, d):
    o = 3 * NA_W + MLA_Q_RANK + MLA_KV_RANK
    kpe = w[:, o:o + MLA_ROPE_DIM]
    o2 = o + MLA_ROPE_DIM
    diff = w[:, o2:o2 + 3 * DIFF_W]
    gates = w[:, o2 + 3 * DIFF_W:]
    pad = jnp.zeros((w.shape[0], LANES - MLA_ROPE_DIM), w.dtype)
    return jnp.concatenate([w[:, :o], gates, diff, kpe, pad], axis=1).astype(BF16)


def _softmax_pv(parts):
    mx = parts[0][0].max(axis=-1, keepdims=True)
    for s, _ in parts[1:]:
        mx = jnp.maximum(mx, s.max(axis=-1, keepdims=True))
    acc = None
    den = None
    for s, v in parts:
        e = jnp.exp(s - mx)
        den_i = e.sum(axis=-1, keepdims=True)
        o_i = _dot(e.astype(BF16), v)
        acc = o_i if acc is None else acc + o_i
        den = den_i if den is None else den + den_i
    return acc * (1.0 / den)


def _na_kernel(q_ref, k_ref, v_ref, kc_ref, vc_ref, bias_ref, o_ref, *, rows_per_step, n_rows):
    i = pl.program_id(1)
    kh = min(NA_KH, n_rows)
    n_loc = kh * GRID_W

    def row_body(rr, carry):
        r = i * rows_per_step + rr
        r0 = jnp.clip(r - kh // 2, 0, n_rows - kh)
        dsel = r0 - r + (NA_KH - 1)
        q0 = pl.multiple_of(rr * GRID_W, GRID_W)
        k0 = pl.multiple_of(r0 * GRID_W, GRID_W)
        for h in range(N_HEADS_NA):
            cs = slice(h * HEAD_DIM, (h + 1) * HEAD_DIM)
            q = q_ref[pl.ds(q0, GRID_W), cs]
            k = k_ref[pl.ds(k0, n_loc), cs]
            v = v_ref[pl.ds(k0, n_loc), cs]
            s_loc = _dot_nt(q, k) * NA_SCALE + bias_ref[dsel, h]
            s_ctx = _dot_nt(q, kc_ref[:, cs]) * NA_SCALE
            o = _softmax_pv([(s_loc, v), (s_ctx, vc_ref[:, cs])])
            o_ref[pl.ds(q0, GRID_W), cs] = o.astype(BF16)
        return carry

    lax.fori_loop(0, rows_per_step, row_body, 0)


def _na_bias(rpb, n_rows):
    kh = min(NA_KH, n_rows)
    h, n_ro, _ = rpb.shape
    w = GRID_W
    col = jnp.arange(w)
    c0 = jnp.clip(col - NA_KW // 2, 0, w - NA_KW)
    col_ok = (col[None, :] >= c0[:, None]) & (col[None, :] < c0[:, None] + NA_KW)
    edge = w - NA_KW
    rpb = rpb.astype(F32)
    text = jnp.concatenate([jnp.broadcast_to(rpb[..., :1], (h, n_ro, edge)), rpb,
                            jnp.broadcast_to(rpb[..., -1:], (h, n_ro, edge)),
                            jnp.zeros((h, n_ro, 1), F32)], axis=-1)
    skew = jnp.tile(text, (1, 1, w))[..., :w * (2 * w - 1)].reshape(h, n_ro, w, 2 * w - 1)
    t = jnp.where(col_ok[None, None], skew[..., w - 1:], NEG_INF)
    per_d = [jnp.swapaxes(t[:, d:d + kh], 1, 2).reshape(h, w, kh * w) for d in range(NA_KH)]
    return jnp.stack(per_d, axis=0)


def _na_attn(px, pc, bias, cols, *, batch, seq, ctx):
    n_rows = seq // GRID_W
    rows_per_step = _pick(n_rows, (8, 4, 2, 1))
    steps = n_rows // rows_per_step
    tq = rows_per_step * GRID_W
    qb, kb, vb = cols.na_q // NA_W, cols.na_k // NA_W, cols.na_v // NA_W
    return pl.pallas_call(
        functools.partial(_na_kernel, rows_per_step=rows_per_step, n_rows=n_rows),
        grid=(batch, steps),
        in_specs=[
            pl.BlockSpec((tq, NA_W), lambda b, i: (b * steps + i, qb)),
            pl.BlockSpec((seq, NA_W), lambda b, i: (b, kb)),
            pl.BlockSpec((seq, NA_W), lambda b, i: (b, vb)),
            pl.BlockSpec((ctx, NA_W), lambda b, i: (b, kb)),
            pl.BlockSpec((ctx, NA_W), lambda b, i: (b, vb)),
            pl.BlockSpec(bias.shape, lambda b, i: (0, 0, 0, 0)),
        ],
        out_specs=pl.BlockSpec((tq, NA_W), lambda b, i: (b * steps + i, 0)),
        out_shape=jax.ShapeDtypeStruct((batch * seq, NA_W), BF16),
        compiler_params=_params("parallel", "arbitrary"),
        name="na_attn",
    )(px, px, px, pc, pc, bias)


def _ctx_na_kernel(q_ref, k_ref, v_ref, o_ref):
    for h in range(N_HEADS_NA):
        cs = slice(h * HEAD_DIM, (h + 1) * HEAD_DIM)
        s = _dot_nt(q_ref[:, cs], k_ref[:, cs]) * NA_SCALE
        o_ref[:, cs] = _softmax_pv([(s, v_ref[:, cs])]).astype(BF16)


def _ctx_na_attn(pc, cols, *, batch, ctx):
    qb, kb, vb = cols.na_q // NA_W, cols.na_k // NA_W, cols.na_v // NA_W
    return pl.pallas_call(
        _ctx_na_kernel,
        grid=(batch,),
        in_specs=[
            pl.BlockSpec((ctx, NA_W), lambda b: (b, qb)),
            pl.BlockSpec((ctx, NA_W), lambda b: (b, kb)),
            pl.BlockSpec((ctx, NA_W), lambda b: (b, vb)),
        ],
        out_specs=pl.BlockSpec((ctx, NA_W), lambda b: (b, 0)),
        out_shape=jax.ShapeDtypeStruct((batch * ctx, NA_W), BF16),
        compiler_params=_params("parallel"),
        name="ctx_na_attn",
    )(pc, pc, pc)


def _mla_kernel(*refs, ctx, seq, latent):
    if latent:
        (cq_ref, ckvc_ref, kpec_ref, ckvl_ref, kpel_ref, cosq_ref, sinq_ref, cosk_ref, sink_ref,
         qn_ref, kvn_ref, wuq_ref, wukv_ref, o_ref, k_scr, v_scr) = refs
    else:
        cq_ref, ckvc_ref, kpec_ref, qn_ref, kvn_ref, wuq_ref, wukv_ref, o_ref, k_scr, v_scr = refs

    @pl.when(pl.program_id(2) == 0)
    def _():
        w = wukv_ref[...]

        def fill(row0, n, ckv_ref, kpe):
            kv = _dot(_rms(ckv_ref[...].astype(F32), kvn_ref[...]).astype(BF16), w)
            k_scr[row0:row0 + n, 0:MLA_NOPE_DIM] = kv[:, :MLA_NOPE_DIM].astype(BF16)
            k_scr[row0:row0 + n, MLA_NOPE_DIM:] = kpe.astype(BF16)
            v_scr[row0:row0 + n, :] = kv[:, MLA_NOPE_DIM:].astype(BF16)

        fill(0, ctx, ckvc_ref, kpec_ref[...])
        if latent:
            fill(ctx, seq, ckvl_ref, _rope(kpel_ref[...].astype(F32), cosk_ref[...], sink_ref[...]))

    q = _dot(_rms(cq_ref[...].astype(F32), qn_ref[...]).astype(BF16), wuq_ref[...])
    q_pe = q[:, MLA_NOPE_DIM:]
    if latent:
        q_pe = _rope(q_pe, cosq_ref[...], sinq_ref[...])
    q = jnp.concatenate([q[:, :MLA_NOPE_DIM], q_pe], axis=1).astype(BF16)
    s = _dot_nt(q, k_scr[...]) * MLA_SCALE
    o_ref[...] = _softmax_pv([(s, v_scr[...])]).astype(BF16)


def _mla_attn(pq, pc, px, cos, sin, qn, kvn, wuq, wukv, cols, *, batch, n_q, ctx, seq, latent):
    tq = _pick(n_q, (256, 128))
    nq = n_q // tq
    cqb, ckvb, kpeb = cols.cq // MLA_Q_RANK, cols.ckv // MLA_KV_RANK, cols.kpe // LANES
    n_keys = ctx + (seq if latent else 0)
    in_specs = [
        pl.BlockSpec((tq, MLA_Q_RANK), lambda b, h, i: (b * nq + i, cqb)),
        pl.BlockSpec((ctx, MLA_KV_RANK), lambda b, h, i: (b, ckvb)),
        pl.BlockSpec((ctx, LANES), lambda b, h, i: (b, kpeb)),
    ]
    args = [pq, pc, pc]
    if latent:
        in_specs += [
            pl.BlockSpec((seq, MLA_KV_RANK), lambda b, h, i: (b, ckvb)),
            pl.BlockSpec((seq, LANES), lambda b, h, i: (b, kpeb)),
            pl.BlockSpec((tq, LANES), lambda b, h, i: (i, 0)),
            pl.BlockSpec((tq, LANES), lambda b, h, i: (i, 0)),
            pl.BlockSpec((seq, LANES), lambda b, h, i: (0, 0)),
            pl.BlockSpec((seq, LANES), lambda b, h, i: (0, 0)),
        ]
        args += [px, px, cos, sin, cos, sin]
    in_specs += [
        pl.BlockSpec((1, MLA_Q_RANK), lambda b, h, i: (0, 0)),
        pl.BlockSpec((1, MLA_KV_RANK), lambda b, h, i: (0, 0)),
        pl.BlockSpec((None, MLA_Q_RANK, 2 * LANES), lambda b, h, i: (h, 0, 0)),
        pl.BlockSpec((None, MLA_KV_RANK, 2 * LANES), lambda b, h, i: (h, 0, 0)),
    ]
    args += [qn, kvn, wuq, wukv]
    return pl.pallas_call(
        functools.partial(_mla_kernel, ctx=ctx, seq=seq, latent=latent),
        grid=(batch, N_HEADS_MLA, nq),
        in_specs=in_specs,
        out_specs=pl.BlockSpec((tq, MLA_V_DIM), lambda b, h, i: (b * nq + i, h)),
        out_shape=jax.ShapeDtypeStruct((batch * n_q, MLA_W), BF16),
        scratch_shapes=[pltpu.VMEM((n_keys, 2 * LANES), BF16), pltpu.VMEM((n_keys, MLA_V_DIM), BF16)],
        compiler_params=_params("parallel", "parallel", "arbitrary"),
        name="mla_attn" if latent else "ctx_mla_attn",
    )(*args)


def _diff_kernel(*refs, ctx, seq, latent, lambda_init):
    if latent:
        (q_ref, kc_ref, vc_ref, kl_ref, vl_ref, cosq_ref, sinq_ref, cosk_ref, sink_ref,
         lam_ref, sub_ref, o_ref, k_scr, v_scr) = refs
    else:
        q_ref, kc_ref, vc_ref, lam_ref, sub_ref, o_ref, k_scr, v_scr = refs

    @pl.when(pl.program_id(2) == 0)
    def _():
        k_scr[0:ctx, :] = kc_ref[...]
        v_scr[0:ctx, :] = vc_ref[...]
        if latent:
            k_scr[ctx:ctx + seq, :] = _rope(kl_ref[...].astype(F32), cosk_ref[...], sink_ref[...]).astype(BF16)
            v_scr[ctx:ctx + seq, :] = vl_ref[...]

    dl = lam_ref[...]
    lam = (jnp.exp(jnp.sum(dl[0:1] * dl[1:2], axis=-1, keepdims=True))
           - jnp.exp(jnp.sum(dl[2:3] * dl[3:4], axis=-1, keepdims=True)) + lambda_init)

    q = q_ref[...].astype(F32)
    if latent:
        q = _rope(q, cosq_ref[...], sinq_ref[...])
    lane = lax.broadcasted_iota(jnp.int32, q.shape, 1)
    q1 = jnp.where(lane < DIFF_QK_DIM, q, 0.0).astype(BF16)
    q2 = jnp.where(lane >= DIFF_QK_DIM, q, 0.0).astype(BF16)
    k = k_scr[...]

    def probs(qm):
        s = _dot_nt(qm, k) * DIFF_SCALE
        e = jnp.exp(s - s.max(axis=-1, keepdims=True))
        return e, 1.0 / e.sum(axis=-1, keepdims=True)

    e1, r1 = probs(q1)
    e2, r2 = probs(q2)
    w = (e1 * r1 - e2 * (lam * r2)).astype(BF16)
    o = _dot(w, v_scr[...])
    o_ref[...] = (_rms(o, sub_ref[...]) * (1.0 - lambda_init)).astype(BF16)


def _diff_attn(pq, pc, px, cos, sin, dlam, subln, cols, *, batch, n_q, ctx, seq, latent, lambda_init):
    tq = _pick(n_q, (256, 128))
    nq = n_q // tq
    qb, kb, vb = cols.dq // HEAD_DIM, cols.dk // HEAD_DIM, cols.dv // HEAD_DIM
    n_keys = ctx + (seq if latent else 0)
    in_specs = [
        pl.BlockSpec((tq, HEAD_DIM), lambda b, h, i: (b * nq + i, qb + h)),
        pl.BlockSpec((ctx, HEAD_DIM), lambda b, h, i: (b, kb + h)),
        pl.BlockSpec((ctx, HEAD_DIM), lambda b, h, i: (b, vb + h)),
    ]
    args = [pq, pc, pc]
    if latent:
        in_specs += [
            pl.BlockSpec((seq, HEAD_DIM), lambda b, h, i: (b, kb + h)),
            pl.BlockSpec((seq, HEAD_DIM), lambda b, h, i: (b, vb + h)),
            pl.BlockSpec((tq, LANES), lambda b, h, i: (i, 0)),
            pl.BlockSpec((tq, LANES), lambda b, h, i: (i, 0)),
            pl.BlockSpec((seq, LANES), lambda b, h, i: (0, 0)),
            pl.BlockSpec((seq, LANES), lambda b, h, i: (0, 0)),
        ]
        args += [px, px, cos, sin, cos, sin]
    in_specs += [
        pl.BlockSpec((4, DIFF_QK_DIM), lambda b, h, i: (0, 0)),
        pl.BlockSpec((1, HEAD_DIM), lambda b, h, i: (0, 0)),
    ]
    args += [dlam, subln]
    return pl.pallas_call(
        functools.partial(_diff_kernel, ctx=ctx, seq=seq, latent=latent, lambda_init=lambda_init),
        grid=(batch, N_HEADS_DIFF, nq),
        in_specs=in_specs,
        out_specs=pl.BlockSpec((tq, HEAD_DIM), lambda b, h, i: (b * nq + i, h)),
        out_shape=jax.ShapeDtypeStruct((batch * n_q, DIFF_W), BF16),
        scratch_shapes=[pltpu.VMEM((n_keys, HEAD_DIM), BF16), pltpu.VMEM((n_keys, HEAD_DIM), BF16)],
        compiler_params=_params("parallel", "parallel", "arbitrary"),
        name="diff_attn" if latent else "ctx_diff_attn",
    )(*args)


def _merge_kernel(oa_ref, ob_ref, od_ref, ga_ref, gb_ref, gd_ref, wa_ref, wb_ref, wd_ref, y_ref):
    y = (_sigmoid(ga_ref[...].astype(F32)) * _dot(oa_ref[...], wa_ref[...])
         + _sigmoid(gb_ref[...].astype(F32)) * _dot(ob_ref[...], wb_ref[...])
         + _sigmoid(gd_ref[...].astype(F32)) * _dot(od_ref[...], wd_ref[...]))
    y_ref[...] = y.astype(BF16)


def _merge(o_na, o_mla, o_diff, p, wb_na, wb_mla, wb_diff, cols, *, tm):
    n = o_na.shape[0]
    d = wb_na.shape[1]
    tn = _pick(d, (512, 256, 128))
    g0 = cols.gates // tn
    gs = d // tn
    return pl.pallas_call(
        _merge_kernel,
        grid=(n // tm, d // tn),
        in_specs=[
            pl.BlockSpec((tm, NA_W), lambda i, j: (i, 0)),
            pl.BlockSpec((tm, MLA_W), lambda i, j: (i, 0)),
            pl.BlockSpec((tm, DIFF_W), lambda i, j: (i, 0)),
            pl.BlockSpec((tm, tn), lambda i, j: (i, g0 + j)),
            pl.BlockSpec((tm, tn), lambda i, j: (i, g0 + gs + j)),
            pl.BlockSpec((tm, tn), lambda i, j: (i, g0 + 2 * gs + j)),
            pl.BlockSpec((NA_W, tn), lambda i, j: (0, j)),
            pl.BlockSpec((MLA_W, tn), lambda i, j: (0, j)),
            pl.BlockSpec((DIFF_W, tn), lambda i, j: (0, j)),
        ],
        out_specs=pl.BlockSpec((tm, tn), lambda i, j: (i, j)),
        out_shape=jax.ShapeDtypeStruct((n, d), BF16),
        compiler_params=_params("parallel", "parallel"),
        name="merge",
    )(o_na, o_mla, o_diff, p, p, p, wb_na, wb_mla, wb_diff)


def _outproj_kernel(x_ref, y_ref, mod_ref, w_ref, o_ref):
    o_ref[...] = x_ref[...] + mod_ref[...] * _dot(y_ref[...], w_ref[...].astype(BF16))


def _outproj(x, y, gate, w_out, *, l, rows_per_group, tm):
    n, d = x.shape
    tn = _pick(d, (512, 256, 128))
    tiles_per_group = rows_per_group // tm
    return pl.pallas_call(
        _outproj_kernel,
        grid=(n // tm, d // tn),
        in_specs=[
            pl.BlockSpec((tm, tn), lambda i, j: (i, j)),
            pl.BlockSpec((tm, d), lambda i, j: (i, 0)),
            pl.BlockSpec((None, 1, tn), lambda i, j: (i // tiles_per_group, 0, j)),
            pl.BlockSpec((None, d, tn), lambda i, j: (l, 0, j)),
        ],
        out_specs=pl.BlockSpec((tm, tn), lambda i, j: (i, j)),
        out_shape=jax.ShapeDtypeStruct((n, d), F32),
        compiler_params=_params("parallel", "parallel"),
        name="outproj",
    )(x, y, gate, w_out)


def _rope_tables(seq):
    quarter = MLA_ROPE_DIM // 4
    freqs = ROPE_THETA ** (-jnp.arange(quarter, dtype=F32) / quarter)
    t = jnp.arange(seq)
    rows, cols = t // GRID_W, t % GRID_W

    def unit(pos):
        ang = pos.astype(F32)[:, None] * freqs
        c, s = jnp.cos(ang), jnp.sin(ang)
        return jnp.concatenate([c, c], axis=1), jnp.concatenate([-s, s], axis=1)

    cr, sr = unit(rows)
    cc, sc = unit(cols)
    cos = jnp.concatenate([cr, cc], axis=1)
    sin = jnp.concatenate([sr, sc], axis=1)
    reps = LANES // MLA_ROPE_DIM
    return jnp.tile(cos, (1, reps)), jnp.tile(sin, (1, reps))


def _pack_heads(w, n_heads, width):
    k = w.shape[0]
    per = w.shape[1] // n_heads
    w = jnp.moveaxis(w.reshape(k, n_heads, per), 1, 0)
    if per < width:
        w = jnp.concatenate([w, jnp.zeros((n_heads, k, width - per), w.dtype)], axis=2)
    return w.astype(BF16)


def kernel(x, c, ctx, c_ctx, w_ada, b_ada, norm_w, ffn_w_in, ffn_w_out, w_in, na_rpb, mla_q_norm, mla_kv_norm,
           mla_w_uq, mla_w_ukv, diff_lambda, diff_subln, w_branch, w_out, final_norm):
    batch, seq, d = x.shape
    n_ctx = ctx.shape[1]
    depth = w_ada.shape[0]
    cols = _Cols(d)
    tm_x = _pick(seq, (1024, 512, 256, 128))
    tm_c = batch * n_ctx

    xs = x.reshape(batch * seq, d)
    xc = ctx.reshape(batch * n_ctx, d)
    cvec = jnp.concatenate([c, c_ctx[None, :], jnp.zeros((8 - batch - 1, d), F32)], axis=0)
    mods = _ada(cvec, w_ada, b_ada).reshape(depth, 8, N_MOD, d)
    cos, sin = _rope_tables(seq)

    for l in range(depth):
        last = l == depth - 1
        lambda_init = 0.8 - 0.6 * math.exp(-0.3 * l)
        m = mods[l, :batch]
        mc = mods[l, batch:batch + 1]
        nw = norm_w[l].reshape(3, 1, d)

        xs = _ffn(xs, m, nw[0], ffn_w_in, ffn_w_out, l=l, s=0, mod0=0, rows_per_group=seq, tm=tm_x)
        xc = _ffn(xc, mc, nw[0], ffn_w_in, ffn_w_out, l=l, s=0, mod0=0, rows_per_group=tm_c, tm=tm_c)

        wp = _pack_w_in(w_in[l], d)
        px = _inproj(xs, m, nw[1], wp, rows_per_group=seq, tm=tm_x)
        pc = _inproj(xc, mc, nw[1], wp, rows_per_group=tm_c, tm=tm_c)

        bias = _na_bias(na_rpb[l], seq // GRID_W)
        qn = mla_q_norm[l].reshape(1, MLA_Q_RANK)
        kvn = mla_kv_norm[l].reshape(1, MLA_KV_RANK)
        wuq = _pack_heads(mla_w_uq[l], N_HEADS_MLA, 2 * LANES)
        wukv = _pack_heads(mla_w_ukv[l], N_HEADS_MLA, 2 * LANES)
        subln = diff_subln[l].reshape(1, HEAD_DIM)
        wb = w_branch[l].astype(BF16)
        wb_na, wb_mla, wb_diff = wb[:NA_W], wb[NA_W:NA_W + MLA_W], wb[NA_W + MLA_W:]

        o_na = _na_attn(px, pc, bias, cols, batch=batch, seq=seq, ctx=n_ctx)
        o_mla = _mla_attn(px, pc, px, cos, sin, qn, kvn, wuq, wukv, cols,
                          batch=batch, n_q=seq, ctx=n_ctx, seq=seq, latent=True)
        o_diff = _diff_attn(px, pc, px, cos, sin, diff_lambda[l], subln, cols,
                            batch=batch, n_q=seq, ctx=n_ctx, seq=seq, latent=True, lambda_init=lambda_init)
        y = _merge(o_na, o_mla, o_diff, px, wb_na, wb_mla, wb_diff, cols, tm=tm_x)
        xs = _outproj(xs, y, m[:, 5:6], w_out, l=l, rows_per_group=seq, tm=tm_x)
        xs = _ffn(xs, m, nw[2], ffn_w_in, ffn_w_out, l=l, s=1, mod0=6, rows_per_group=seq, tm=tm_x)

        if not last:
            oc_na = _ctx_na_attn(pc, cols, batch=batch, ctx=n_ctx)
            oc_mla = _mla_attn(pc, pc, px, cos, sin, qn, kvn, wuq, wukv, cols,
                               batch=batch, n_q=n_ctx, ctx=n_ctx, seq=seq, latent=False)
            oc_diff = _diff_attn(pc, pc, px, cos, sin, diff_lambda[l], subln, cols,
                                 batch=batch, n_q=n_ctx, ctx=n_ctx, seq=seq, latent=False, lambda_init=lambda_init)
            yc = _merge(oc_na, oc_mla, oc_diff, pc, wb_na, wb_mla, wb_diff, cols, tm=tm_c)
            xc = _outproj(xc, yc, mc[:, 5:6], w_out, l=l, rows_per_group=tm_c, tm=tm_c)
            xc = _ffn(xc, mc, nw[2], ffn_w_in, ffn_w_out, l=l, s=1, mod0=6, rows_per_group=tm_c, tm=tm_c)

    return _final_norm(xs, final_norm.reshape(1, d), tm=tm_x).reshape(batch, seq, d)
```

```python
import functools
import math

import jax
import jax.numpy as jnp
from jax import lax
from jax.experimental import pallas as pl
from jax.experimental.pallas import tpu as pltpu

GRID_W = 64
HEAD_DIM = 128
N_HEADS_NA = 6
N_HEADS_MLA = 5
N_HEADS_DIFF = 5
NA_KH = 8
NA_KW = 16
MLA_Q_RANK = 768
MLA_KV_RANK = 512
MLA_NOPE_DIM = 128
MLA_ROPE_DIM = 64
MLA_V_DIM = 128
DIFF_QK_DIM = HEAD_DIM // 2
N_MOD = 9
ROPE_THETA = 10000.0
NORM_EPS = 1e-6
NEG_INF = -1e30
NA_W = N_HEADS_NA * HEAD_DIM
MLA_W = N_HEADS_MLA * MLA_V_DIM
DIFF_W = N_HEADS_DIFF * HEAD_DIM
NA_SCALE = HEAD_DIM ** -0.5
MLA_SCALE = (MLA_NOPE_DIM + MLA_ROPE_DIM) ** -0.5
DIFF_SCALE = DIFF_QK_DIM ** -0.5

LOG2E = 1.4426950408889634
LANES = 128
VMEM_LIMIT = 60 * 1024 * 1024
BF16 = jnp.bfloat16
F32 = jnp.float32


def _params(*sem):
    return pltpu.CompilerParams(dimension_semantics=sem, vmem_limit_bytes=VMEM_LIMIT)


def _pick(n, prefs):
    for p in prefs:
        if n % p == 0:
            return p
    return n


def _dot(a, b):
    return jnp.dot(a, b, preferred_element_type=F32)


def _dot_nt(a, b):
    return lax.dot_general(a, b, (((1,), (1,)), ((), ())), preferred_element_type=F32)


def _rms(x, g):
    return x * lax.rsqrt(jnp.mean(x * x, axis=-1, keepdims=True) + NORM_EPS) * g


def _silu(x):
    return x * (1.0 / (1.0 + jnp.exp(-x)))


def _sigmoid(x):
    return 1.0 / (1.0 + jnp.exp(-x))


def _swap16(x):
    lane = lax.broadcasted_iota(jnp.int32, x.shape, x.ndim - 1)
    down = pltpu.roll(x, 16, x.ndim - 1)
    up = pltpu.roll(x, LANES - 16, x.ndim - 1)
    return jnp.where((lane & 16) != 0, down, up)


def _rope(x, cos, sin):
    return x * cos + _swap16(x) * sin


def _ada_kernel(c_ref, w_ref, b_ref, o_ref):
    a = _silu(c_ref[...]).astype(BF16)
    o_ref[...] = _dot(a, w_ref[...].astype(BF16)) + b_ref[...]


def _ada(cvec, w_ada, b_ada):
    depth, d, n = w_ada.shape
    tn = _pick(n, (1024, 512, 256, 128))
    return pl.pallas_call(
        _ada_kernel,
        grid=(depth, n // tn),
        in_specs=[
            pl.BlockSpec((8, d), lambda l, j: (0, 0)),
            pl.BlockSpec((None, d, tn), lambda l, j: (l, 0, j)),
            pl.BlockSpec((None, 1, tn), lambda l, j: (l, 0, j)),
        ],
        out_specs=pl.BlockSpec((None, 8, tn), lambda l, j: (l, 0, j)),
        out_shape=jax.ShapeDtypeStruct((depth, 8, n), F32),
        compiler_params=_params("parallel", "parallel"),
        name="ada",
    )(cvec, w_ada, b_ada.reshape(depth, 1, n))


def _ffn_in_kernel(x_ref, mod_ref, nw_ref, wa_ref, wb_ref, g_ref, h_ref, *, mod0):
    @pl.when(pl.program_id(1) == 0)
    def _():
        shift = mod_ref[mod0:mod0 + 1, :]
        scale = mod_ref[mod0 + 1:mod0 + 2, :]
        h_ref[...] = (_rms(x_ref[...], nw_ref[...]) * (1.0 + scale) + shift).astype(BF16)

    h = h_ref[...]
    a = _dot(h, wa_ref[...].astype(BF16))
    b = _dot(h, wb_ref[...].astype(BF16))
    g_ref[...] = (_silu(a) * b).astype(BF16)


def _ffn_out_kernel(x_ref, g_ref, mod_ref, w_ref, o_ref, *, mod0):
    gate = 0.5 * mod_ref[mod0 + 2:mod0 + 3, :]
    o_ref[...] = x_ref[...] + gate * _dot(g_ref[...], w_ref[...].astype(BF16))


def _ffn(x, mod, nw, w_in, w_out, *, l, s, mod0, rows_per_group, tm):
    n, d = x.shape
    f = w_out.shape[2]
    fc = _pick(f, (512, 256, 128))
    nf = f // fc
    tn = _pick(d, (256, 128))
    tiles_per_group = rows_per_group // tm
    g = pl.pallas_call(
        functools.partial(_ffn_in_kernel, mod0=mod0),
        grid=(n // tm, nf),
        in_specs=[
            pl.BlockSpec((tm, d), lambda i, j: (i, 0)),
            pl.BlockSpec((None, N_MOD, d), lambda i, j: (i // tiles_per_group, 0, 0)),
            pl.BlockSpec((1, d), lambda i, j: (0, 0)),
            pl.BlockSpec((None, None, d, fc), lambda i, j: (l, s, 0, j)),
            pl.BlockSpec((None, None, d, fc), lambda i, j: (l, s, 0, nf + j)),
        ],
        out_specs=pl.BlockSpec((tm, fc), lambda i, j: (i, j)),
        out_shape=jax.ShapeDtypeStruct((n, f), BF16),
        scratch_shapes=[pltpu.VMEM((tm, d), BF16)],
        compiler_params=_params("parallel", "arbitrary"),
        name="ffn_in",
    )(x, mod, nw, w_in, w_in)
    return pl.pallas_call(
        functools.partial(_ffn_out_kernel, mod0=mod0),
        grid=(n // tm, d // tn),
        in_specs=[
            pl.BlockSpec((tm, tn), lambda i, j: (i, j)),
            pl.BlockSpec((tm, f), lambda i, j: (i, 0)),
            pl.BlockSpec((None, N_MOD, tn), lambda i, j: (i // tiles_per_group, 0, j)),
            pl.BlockSpec((None, None, f, tn), lambda i, j: (l, s, 0, j)),
        ],
        out_specs=pl.BlockSpec((tm, tn), lambda i, j: (i, j)),
        out_shape=jax.ShapeDtypeStruct((n, d), F32),
        compiler_params=_params("parallel", "parallel"),
        name="ffn_out",
    )(x, g, mod, w_out)


def _final_norm_kernel(x_ref, g_ref, o_ref):
    o_ref[...] = _rms(x_ref[...], g_ref[...])


def _final_norm(x, g, *, tm):
    n, d = x.shape
    return pl.pallas_call(
        _final_norm_kernel,
        grid=(n // tm,),
        in_specs=[pl.BlockSpec((tm, d), lambda i: (i, 0)), pl.BlockSpec((1, d), lambda i: (0, 0))],
        out_specs=pl.BlockSpec((tm, d), lambda i: (i, 0)),
        out_shape=jax.ShapeDtypeStruct((n, d), F32),
        compiler_params=_params("parallel"),
        name="final_norm",
    )(x, g)


def _inproj_kernel(x_ref, mod_ref, nw_ref, w_ref, o_ref, h_ref):
    @pl.when(pl.program_id(1) == 0)
    def _():
        shift = mod_ref[3:4, :]
        scale = mod_ref[4:5, :]
        h_ref[...] = (_rms(x_ref[...], nw_ref[...]) * (1.0 + scale) + shift).astype(BF16)

    o_ref[...] = _dot(h_ref[...], w_ref[...]).astype(BF16)


def _inproj(x, mod, nw, w, *, rows_per_group, tm):
    n, d = x.shape
    nw_cols = w.shape[1]
    tn = _pick(nw_cols, (512, 256, 128))
    tiles_per_group = rows_per_group // tm
    return pl.pallas_call(
        _inproj_kernel,
        grid=(n // tm, nw_cols // tn),
        in_specs=[
            pl.BlockSpec((tm, d), lambda i, j: (i, 0)),
            pl.BlockSpec((None, N_MOD, d), lambda i, j: (i // tiles_per_group, 0, 0)),
            pl.BlockSpec((1, d), lambda i, j: (0, 0)),
            pl.BlockSpec((d, tn), lambda i, j: (0, j)),
        ],
        out_specs=pl.BlockSpec((tm, tn), lambda i, j: (i, j)),
        out_shape=jax.ShapeDtypeStruct((n, nw_cols), BF16),
        scratch_shapes=[pltpu.VMEM((tm, d), BF16)],
        compiler_params=_params("parallel", "arbitrary"),
        name="inproj",
    )(x, mod, nw, w)


class _Cols:
    def __init__(self, d):
        self.na_q = 0
        self.na_k = NA_W
        self.na_v = 2 * NA_W
        self.cq = 3 * NA_W
        self.ckv = self.cq + MLA_Q_RANK
        self.gates = self.ckv + MLA_KV_RANK
        self.dq = self.gates + 3 * d
        self.dk = self.dq + DIFF_W
        self.dv = self.dk + DIFF_W
        self.kpe = self.dv + DIFF_W
        self.total = self.kpe + LANES


def _pack_w_in(w, d):
    o = 3 * NA_W + MLA_Q_RANK + MLA_KV_RANK
    kpe = w[:, o:o + MLA_ROPE_DIM]
    o2 = o + MLA_ROPE_DIM
    diff = w[:, o2:o2 + 3 * DIFF_W]
    gates = w[:, o2 + 3 * DIFF_W:]
    pad = jnp.zeros((w.shape[0], LANES - MLA_ROPE_DIM), w.dtype)
    return jnp.concatenate([w[:, :o], gates, diff, kpe, pad], axis=1).astype(BF16)


def _softmax_pv(parts):
    mx = parts[0][0].max(axis=-1, keepdims=True)
    for s, _ in parts[1:]:
        mx = jnp.maximum(mx, s.max(axis=-1, keepdims=True))
    acc = None
    den = None
    for s, v in parts:
        e = jnp.exp(s - mx)
        den_i = e.sum(axis=-1, keepdims=True)
        o_i = _dot(e.astype(BF16), v)
        acc = o_i if acc is None else acc + o_i
        den = den_i if den is None else den + den_i
    return acc * (1.0 / den)


NA_ROWS = 4
NA_UNION = NA_ROWS + NA_KH


def _na_union_start(r, n_rows):
    if isinstance(r, int):
        return min(min(max(r - NA_KH // 2, 0), n_rows - NA_KH), n_rows - NA_UNION)
    return jnp.minimum(jnp.clip(r - NA_KH // 2, 0, n_rows - NA_KH), n_rows - NA_UNION)


def _na_kernel(q_ref, k_ref, v_ref, kc_ref, vc_ref, bias_ref, o_ref, *, n_rows):
    u0 = _na_union_start(pl.program_id(1) * NA_ROWS, n_rows)
    k0 = pl.multiple_of(u0 * GRID_W, GRID_W)
    n_loc = NA_UNION * GRID_W
    c = NA_SCALE * LOG2E
    cs = [slice(h * HEAD_DIM, (h + 1) * HEAD_DIM) for h in range(N_HEADS_NA)]
    s_loc = [_dot_nt(q_ref[:, cs[h]], k_ref[pl.ds(k0, n_loc), cs[h]]) * c + bias_ref[h] for h in range(N_HEADS_NA)]
    s_ctx = [_dot_nt(q_ref[:, cs[h]], kc_ref[:, cs[h]]) * c for h in range(N_HEADS_NA)]
    for h in range(N_HEADS_NA):
        m = jnp.maximum(s_loc[h].max(axis=-1, keepdims=True), s_ctx[h].max(axis=-1, keepdims=True))
        e_loc = jnp.exp2(s_loc[h] - m)
        e_ctx = jnp.exp2(s_ctx[h] - m)
        den = e_loc.sum(axis=-1, keepdims=True) + e_ctx.sum(axis=-1, keepdims=True)
        o = _dot(e_loc.astype(BF16), v_ref[pl.ds(k0, n_loc), cs[h]]) + _dot(e_ctx.astype(BF16), vc_ref[:, cs[h]])
        o_ref[:, cs[h]] = (o * (1.0 / den)).astype(BF16)


def _na_bias(rpb, n_rows):
    assert n_rows >= NA_UNION and n_rows % NA_ROWS == 0
    h, n_ro, _ = rpb.shape
    w = GRID_W
    col = jnp.arange(w)
    c0 = jnp.clip(col - NA_KW // 2, 0, w - NA_KW)
    col_ok = (col[None, :] >= c0[:, None]) & (col[None, :] < c0[:, None] + NA_KW)
    edge = w - NA_KW
    rpb = rpb.astype(F32)
    text = jnp.concatenate([jnp.broadcast_to(rpb[..., :1], (h, n_ro, edge)), rpb,
                            jnp.broadcast_to(rpb[..., -1:], (h, n_ro, edge)),
                            jnp.zeros((h, n_ro, 1), F32)], axis=-1)
    skew = jnp.tile(text, (1, 1, w))[..., :w * (2 * w - 1)].reshape(h, n_ro, w, 2 * w - 1)
    t = jnp.where(col_ok[None, None], skew[..., w - 1:], NEG_INF) * LOG2E
    masked = jnp.full((h, w, w), NEG_INF * LOG2E, F32)
    kinds = []
    for r in (0, NA_ROWS, n_rows - NA_ROWS):
        u0 = _na_union_start(r, n_rows)
        q_rows = []
        for qr in range(r, r + NA_ROWS):
            r0 = min(max(qr - NA_KH // 2, 0), n_rows - NA_KH)
            tiles = [t[:, kr - qr + NA_KH - 1] if r0 <= kr < r0 + NA_KH else masked
                     for kr in range(u0, u0 + NA_UNION)]
            q_rows.append(jnp.concatenate(tiles, axis=-1))
        kinds.append(jnp.concatenate(q_rows, axis=1))
    return jnp.stack(kinds, axis=0)


def _na_attn(px, pc, bias, cols, *, batch, seq, ctx):
    n_rows = seq // GRID_W
    steps = n_rows // NA_ROWS
    tq = NA_ROWS * GRID_W
    qb, kb, vb = cols.na_q // NA_W, cols.na_k // NA_W, cols.na_v // NA_W
    kind = lambda i: jnp.where(i == 0, 0, jnp.where(i == steps - 1, 2, 1))
    return pl.pallas_call(
        functools.partial(_na_kernel, n_rows=n_rows),
        grid=(batch, steps),
        in_specs=[
            pl.BlockSpec((tq, NA_W), lambda b, i: (b * steps + i, qb)),
            pl.BlockSpec((seq, NA_W), lambda b, i: (b, kb)),
            pl.BlockSpec((seq, NA_W), lambda b, i: (b, vb)),
            pl.BlockSpec((ctx, NA_W), lambda b, i: (b, kb)),
            pl.BlockSpec((ctx, NA_W), lambda b, i: (b, vb)),
            pl.BlockSpec((None,) + bias.shape[1:], lambda b, i: (kind(i), 0, 0, 0)),
        ],
        out_specs=pl.BlockSpec((tq, NA_W), lambda b, i: (b * steps + i, 0)),
        out_shape=jax.ShapeDtypeStruct((batch * seq, NA_W), BF16),
        compiler_params=_params("parallel", "arbitrary"),
        name="na_attn",
    )(px, px, px, pc, pc, bias)


def _ctx_na_kernel(q_ref, k_ref, v_ref, o_ref):
    for h in range(N_HEADS_NA):
        cs = slice(h * HEAD_DIM, (h + 1) * HEAD_DIM)
        s = _dot_nt(q_ref[:, cs], k_ref[:, cs]) * NA_SCALE
        o_ref[:, cs] = _softmax_pv([(s, v_ref[:, cs])]).astype(BF16)


def _ctx_na_attn(pc, cols, *, batch, ctx):
    qb, kb, vb = cols.na_q // NA_W, cols.na_k // NA_W, cols.na_v // NA_W
    return pl.pallas_call(
        _ctx_na_kernel,
        grid=(batch,),
        in_specs=[
            pl.BlockSpec((ctx, NA_W), lambda b: (b, qb)),
            pl.BlockSpec((ctx, NA_W), lambda b: (b, kb)),
            pl.BlockSpec((ctx, NA_W), lambda b: (b, vb)),
        ],
        out_specs=pl.BlockSpec((ctx, NA_W), lambda b: (b, 0)),
        out_shape=jax.ShapeDtypeStruct((batch * ctx, NA_W), BF16),
        compiler_params=_params("parallel"),
        name="ctx_na_attn",
    )(pc, pc, pc)


def _attn_pipeline_step(qs, k_scr, v_scr, park, parked):
    s_a, m_a = park
    s_b, m_b = parked
    tq = qs[0].shape[0]
    n_keys = k_scr.shape[0]
    ch = _pick(n_keys, (2 * LANES, LANES))
    groups = ch // LANES
    run = [None] * len(qs)
    acc = [None] * len(qs)
    for c in range(n_keys // ch):
        ks = slice(c * ch, (c + 1) * ch)
        k = k_scr[ks, :]
        v = v_scr[ks, :]
        for t, q in enumerate(qs):
            s = _dot_nt(q, k)
            s_a[t, :, ks] = s
            m = s[:, :LANES]
            for g in range(1, groups):
                m = jnp.maximum(m, s[:, g * LANES:(g + 1) * LANES])
            run[t] = m if run[t] is None else jnp.maximum(run[t], m)
            e = jnp.exp2(s_b[t, :, ks] - jnp.concatenate([m_b[t]] * groups, axis=1))
            pv = _dot(e.astype(BF16), v)
            acc[t] = pv if acc[t] is None else acc[t] + pv
    for t in range(len(qs)):
        m_a[t] = jnp.broadcast_to(run[t].max(axis=-1, keepdims=True), (tq, LANES))
    return acc


def _attn_pipeline(step, bufs, body):
    (s0, m0), (s1, m1) = bufs

    @pl.when(step % 2 == 0)
    def _():
        body((s0, m0), (s1, m1))

    @pl.when(step % 2 == 1)
    def _():
        body((s1, m1), (s0, m0))


def _attn_pipeline_init(bufs):
    s1, m1 = bufs[1]
    s1[...] = jnp.zeros(s1.shape, F32)
    m1[...] = jnp.zeros(m1.shape, F32)


def _mla_kernel(*refs, ctx, seq, latent):
    if latent:
        (cq_ref, ckvc_ref, kpec_ref, ckvl_ref, kpel_ref, cosq_ref, sinq_ref, cosk_ref, sink_ref,
         qn_ref, kvn_ref, wuq_ref, wukv_ref, o_ref, k_scr, v_scr, s0, m0, s1, m1) = refs
    else:
        cq_ref, ckvc_ref, kpec_ref, qn_ref, kvn_ref, wuq_ref, wukv_ref, o_ref, k_scr, v_scr, s0, m0, s1, m1 = refs
    step = pl.program_id(2)
    bufs = ((s0, m0), (s1, m1))

    @pl.when(step == 0)
    def _():
        w = wukv_ref[...]

        def fill(row0, n, ckv_ref, kpe):
            kv = _dot(_rms(ckv_ref[...].astype(F32), kvn_ref[...]).astype(BF16), w)
            k_scr[row0:row0 + n, 0:MLA_NOPE_DIM] = kv[:, :MLA_NOPE_DIM].astype(BF16)
            k_scr[row0:row0 + n, MLA_NOPE_DIM:] = kpe.astype(BF16)
            v_scr[row0:row0 + n, 0:MLA_V_DIM] = kv[:, MLA_NOPE_DIM:].astype(BF16)
            v_scr[row0:row0 + n, MLA_V_DIM:] = jnp.ones((n, LANES), BF16)

        fill(0, ctx, ckvc_ref, kpec_ref[...])
        if latent:
            fill(ctx, seq, ckvl_ref, _rope(kpel_ref[...].astype(F32), cosk_ref[...], sink_ref[...]))
        _attn_pipeline_init(bufs)

    def body(park, parked):
        q = _dot(_rms(cq_ref[...].astype(F32), qn_ref[...]).astype(BF16), wuq_ref[...])
        q_pe = q[:, MLA_NOPE_DIM:]
        if latent:
            q_pe = _rope(q_pe, cosq_ref[...], sinq_ref[...])
        q = (jnp.concatenate([q[:, :MLA_NOPE_DIM], q_pe], axis=1) * (MLA_SCALE * LOG2E)).astype(BF16)
        (acc,) = _attn_pipeline_step([q], k_scr, v_scr, park, parked)
        o_ref[...] = (acc[:, :MLA_V_DIM] * (1.0 / acc[:, MLA_V_DIM:])).astype(BF16)

    _attn_pipeline(step, bufs, body)


def _mla_attn(pq, pc, px, cos, sin, qn, kvn, wuq, wukv, cols, *, batch, n_q, ctx, seq, latent):
    tq = _pick(n_q, (256, 128))
    nq = n_q // tq
    cqb, ckvb, kpeb = cols.cq // MLA_Q_RANK, cols.ckv // MLA_KV_RANK, cols.kpe // LANES
    n_keys = ctx + (seq if latent else 0)
    q_tile = lambda i: jnp.minimum(i, nq - 1)
    o_tile = lambda i: jnp.maximum(i - 1, 0)
    in_specs = [
        pl.BlockSpec((tq, MLA_Q_RANK), lambda b, h, i: (b * nq + q_tile(i), cqb)),
        pl.BlockSpec((ctx, MLA_KV_RANK), lambda b, h, i: (b, ckvb)),
        pl.BlockSpec((ctx, LANES), lambda b, h, i: (b, kpeb)),
    ]
    args = [pq, pc, pc]
    if latent:
        in_specs += [
            pl.BlockSpec((seq, MLA_KV_RANK), lambda b, h, i: (b, ckvb)),
            pl.BlockSpec((seq, LANES), lambda b, h, i: (b, kpeb)),
            pl.BlockSpec((tq, LANES), lambda b, h, i: (q_tile(i), 0)),
            pl.BlockSpec((tq, LANES), lambda b, h, i: (q_tile(i), 0)),
            pl.BlockSpec((seq, LANES), lambda b, h, i: (0, 0)),
            pl.BlockSpec((seq, LANES), lambda b, h, i: (0, 0)),
        ]
        args += [px, px, cos, sin, cos, sin]
    in_specs += [
        pl.BlockSpec((1, MLA_Q_RANK), lambda b, h, i: (0, 0)),
        pl.BlockSpec((1, MLA_KV_RANK), lambda b, h, i: (0, 0)),
        pl.BlockSpec((None, MLA_Q_RANK, 2 * LANES), lambda b, h, i: (h, 0, 0)),
        pl.BlockSpec((None, MLA_KV_RANK, 2 * LANES), lambda b, h, i: (h, 0, 0)),
    ]
    args += [qn, kvn, wuq, wukv]
    return pl.pallas_call(
        functools.partial(_mla_kernel, ctx=ctx, seq=seq, latent=latent),
        grid=(batch, N_HEADS_MLA, nq + 1),
        in_specs=in_specs,
        out_specs=pl.BlockSpec((tq, MLA_V_DIM), lambda b, h, i: (b * nq + o_tile(i), h)),
        out_shape=jax.ShapeDtypeStruct((batch * n_q, MLA_W), BF16),
        scratch_shapes=[pltpu.VMEM((n_keys, 2 * LANES), BF16), pltpu.VMEM((n_keys, 2 * LANES), BF16)]
        + 2 * [pltpu.VMEM((1, tq, n_keys), F32), pltpu.VMEM((1, tq, LANES), F32)],
        compiler_params=_params("parallel", "parallel", "arbitrary"),
        name="mla_attn" if latent else "ctx_mla_attn",
    )(*args)


def _diff_kernel(*refs, ctx, seq, latent, lambda_init):
    if latent:
        (q_ref, kc_ref, vc_ref, kl_ref, vl_ref, cosq_ref, sinq_ref, cosk_ref, sink_ref,
         lam_ref, sub_ref, o_ref, k_scr, v_scr, s0, m0, s1, m1) = refs
    else:
        q_ref, kc_ref, vc_ref, lam_ref, sub_ref, o_ref, k_scr, v_scr, s0, m0, s1, m1 = refs
    step = pl.program_id(2)
    bufs = ((s0, m0), (s1, m1))

    @pl.when(step == 0)
    def _():
        k_scr[0:ctx, :] = kc_ref[...]
        v_scr[0:ctx, 0:HEAD_DIM] = vc_ref[...]
        if latent:
            k_scr[ctx:ctx + seq, :] = _rope(kl_ref[...].astype(F32), cosk_ref[...], sink_ref[...]).astype(BF16)
            v_scr[ctx:ctx + seq, 0:HEAD_DIM] = vl_ref[...]
        v_scr[:, HEAD_DIM:] = jnp.ones((v_scr.shape[0], LANES), BF16)
        _attn_pipeline_init(bufs)

    def body(park, parked):
        dl = lam_ref[...]
        lam = (jnp.exp(jnp.sum(dl[0:1] * dl[1:2], axis=-1, keepdims=True))
               - jnp.exp(jnp.sum(dl[2:3] * dl[3:4], axis=-1, keepdims=True)) + lambda_init)
        q = q_ref[...].astype(F32)
        if latent:
            q = _rope(q, cosq_ref[...], sinq_ref[...])
        q = q * (DIFF_SCALE * LOG2E)
        lane = lax.broadcasted_iota(jnp.int32, q.shape, 1)
        q1 = jnp.where(lane < DIFF_QK_DIM, q, 0.0).astype(BF16)
        q2 = jnp.where(lane >= DIFF_QK_DIM, q, 0.0).astype(BF16)
        a1, a2 = _attn_pipeline_step([q1, q2], k_scr, v_scr, park, parked)
        o = a1[:, :HEAD_DIM] * (1.0 / a1[:, HEAD_DIM:]) - a2[:, :HEAD_DIM] * (lam / a2[:, HEAD_DIM:])
        o_ref[...] = (_rms(o, sub_ref[...]) * (1.0 - lambda_init)).astype(BF16)

    _attn_pipeline(step, bufs, body)


def _diff_attn(pq, pc, px, cos, sin, dlam, subln, cols, *, batch, n_q, ctx, seq, latent, lambda_init):
    tq = _pick(n_q, (256, 128))
    nq = n_q // tq
    qb, kb, vb = cols.dq // HEAD_DIM, cols.dk // HEAD_DIM, cols.dv // HEAD_DIM
    n_keys = ctx + (seq if latent else 0)
    q_tile = lambda i: jnp.minimum(i, nq - 1)
    o_tile = lambda i: jnp.maximum(i - 1, 0)
    in_specs = [
        pl.BlockSpec((tq, HEAD_DIM), lambda b, h, i: (b * nq + q_tile(i), qb + h)),
        pl.BlockSpec((ctx, HEAD_DIM), lambda b, h, i: (b, kb + h)),
        pl.BlockSpec((ctx, HEAD_DIM), lambda b, h, i: (b, vb + h)),
    ]
    args = [pq, pc, pc]
    if latent:
        in_specs += [
            pl.BlockSpec((seq, HEAD_DIM), lambda b, h, i: (b, kb + h)),
            pl.BlockSpec((seq, HEAD_DIM), lambda b, h, i: (b, vb + h)),
            pl.BlockSpec((tq, LANES), lambda b, h, i: (q_tile(i), 0)),
            pl.BlockSpec((tq, LANES), lambda b, h, i: (q_tile(i), 0)),
            pl.BlockSpec((seq, LANES), lambda b, h, i: (0, 0)),
            pl.BlockSpec((seq, LANES), lambda b, h, i: (0, 0)),
        ]
        args += [px, px, cos, sin, cos, sin]
    in_specs += [
        pl.BlockSpec((4, DIFF_QK_DIM), lambda b, h, i: (0, 0)),
        pl.BlockSpec((1, HEAD_DIM), lambda b, h, i: (0, 0)),
    ]
    args += [dlam, subln]
    return pl.pallas_call(
        functools.partial(_diff_kernel, ctx=ctx, seq=seq, latent=latent, lambda_init=lambda_init),
        grid=(batch, N_HEADS_DIFF, nq + 1),
        in_specs=in_specs,
        out_specs=pl.BlockSpec((tq, HEAD_DIM), lambda b, h, i: (b * nq + o_tile(i), h)),
        out_shape=jax.ShapeDtypeStruct((batch * n_q, DIFF_W), BF16),
        scratch_shapes=[pltpu.VMEM((n_keys, HEAD_DIM), BF16), pltpu.VMEM((n_keys, 2 * LANES), BF16)]
        + 2 * [pltpu.VMEM((2, tq, n_keys), F32), pltpu.VMEM((2, tq, LANES), F32)],
        compiler_params=_params("parallel", "parallel", "arbitrary"),
        name="diff_attn" if latent else "ctx_diff_attn",
    )(*args)


def _merge_kernel(oa_ref, ob_ref, od_ref, ga_ref, gb_ref, gd_ref, wa_ref, wb_ref, wd_ref, y_ref):
    y = (_sigmoid(ga_ref[...].astype(F32)) * _dot(oa_ref[...], wa_ref[...])
         + _sigmoid(gb_ref[...].astype(F32)) * _dot(ob_ref[...], wb_ref[...])
         + _sigmoid(gd_ref[...].astype(F32)) * _dot(od_ref[...], wd_ref[...]))
    y_ref[...] = y.astype(BF16)


def _merge(o_na, o_mla, o_diff, p, wb_na, wb_mla, wb_diff, cols, *, tm):
    n = o_na.shape[0]
    d = wb_na.shape[1]
    tn = _pick(d, (512, 256, 128))
    g0 = cols.gates // tn
    gs = d // tn
    return pl.pallas_call(
        _merge_kernel,
        grid=(n // tm, d // tn),
        in_specs=[
            pl.BlockSpec((tm, NA_W), lambda i, j: (i, 0)),
            pl.BlockSpec((tm, MLA_W), lambda i, j: (i, 0)),
            pl.BlockSpec((tm, DIFF_W), lambda i, j: (i, 0)),
            pl.BlockSpec((tm, tn), lambda i, j: (i, g0 + j)),
            pl.BlockSpec((tm, tn), lambda i, j: (i, g0 + gs + j)),
            pl.BlockSpec((tm, tn), lambda i, j: (i, g0 + 2 * gs + j)),
            pl.BlockSpec((NA_W, tn), lambda i, j: (0, j)),
            pl.BlockSpec((MLA_W, tn), lambda i, j: (0, j)),
            pl.BlockSpec((DIFF_W, tn), lambda i, j: (0, j)),
        ],
        out_specs=pl.BlockSpec((tm, tn), lambda i, j: (i, j)),
        out_shape=jax.ShapeDtypeStruct((n, d), BF16),
        compiler_params=_params("parallel", "parallel"),
        name="merge",
    )(o_na, o_mla, o_diff, p, p, p, wb_na, wb_mla, wb_diff)


def _outproj_kernel(x_ref, y_ref, mod_ref, w_ref, o_ref):
    o_ref[...] = x_ref[...] + mod_ref[...] * _dot(y_ref[...], w_ref[...].astype(BF16))


def _outproj(x, y, gate, w_out, *, l, rows_per_group, tm):
    n, d = x.shape
    tn = _pick(d, (512, 256, 128))
    tiles_per_group = rows_per_group // tm
    return pl.pallas_call(
        _outproj_kernel,
        grid=(n // tm, d // tn),
        in_specs=[
            pl.BlockSpec((tm, tn), lambda i, j: (i, j)),
            pl.BlockSpec((tm, d), lambda i, j: (i, 0)),
            pl.BlockSpec((None, 1, tn), lambda i, j: (i // tiles_per_group, 0, j)),
            pl.BlockSpec((None, d, tn), lambda i, j: (l, 0, j)),
        ],
        out_specs=pl.BlockSpec((tm, tn), lambda i, j: (i, j)),
        out_shape=jax.ShapeDtypeStruct((n, d), F32),
        compiler_params=_params("parallel", "parallel"),
        name="outproj",
    )(x, y, gate, w_out)


def _rope_tables(seq):
    quarter = MLA_ROPE_DIM // 4
    freqs = ROPE_THETA ** (-jnp.arange(quarter, dtype=F32) / quarter)
    t = jnp.arange(seq)
    rows, cols = t // GRID_W, t % GRID_W

    def unit(pos):
        ang = pos.astype(F32)[:, None] * freqs
        c, s = jnp.cos(ang), jnp.sin(ang)
        return jnp.concatenate([c, c], axis=1), jnp.concatenate([-s, s], axis=1)

    cr, sr = unit(rows)
    cc, sc = unit(cols)
    cos = jnp.concatenate([cr, cc], axis=1)
    sin = jnp.concatenate([sr, sc], axis=1)
    reps = LANES // MLA_ROPE_DIM
    return jnp.tile(cos, (1, reps)), jnp.tile(sin, (1, reps))


def _pack_heads(w, n_heads, width):
    k = w.shape[0]
    per = w.shape[1] // n_heads
    w = jnp.moveaxis(w.reshape(k, n_heads, per), 1, 0)
    if per < width:
        w = jnp.concatenate([w, jnp.zeros((n_heads, k, width - per), w.dtype)], axis=2)
    return w.astype(BF16)


def kernel(x, c, ctx, c_ctx, w_ada, b_ada, norm_w, ffn_w_in, ffn_w_out, w_in, na_rpb, mla_q_norm, mla_kv_norm,
           mla_w_uq, mla_w_ukv, diff_lambda, diff_subln, w_branch, w_out, final_norm):
    batch, seq, d = x.shape
    n_ctx = ctx.shape[1]
    depth = w_ada.shape[0]
    cols = _Cols(d)
    tm_x = _pick(seq, (1024, 512, 256, 128))
    tm_c = batch * n_ctx

    xs = x.reshape(batch * seq, d)
    xc = ctx.reshape(batch * n_ctx, d)
    cvec = jnp.concatenate([c, c_ctx[None, :], jnp.zeros((8 - batch - 1, d), F32)], axis=0)
    mods = _ada(cvec, w_ada, b_ada).reshape(depth, 8, N_MOD, d)
    cos, sin = _rope_tables(seq)

    for l in range(depth):
        last = l == depth - 1
        lambda_init = 0.8 - 0.6 * math.exp(-0.3 * l)
        m = mods[l, :batch]
        mc = mods[l, batch:batch + 1]
        nw = norm_w[l].reshape(3, 1, d)

        xs = _ffn(xs, m, nw[0], ffn_w_in, ffn_w_out, l=l, s=0, mod0=0, rows_per_group=seq, tm=tm_x)
        xc = _ffn(xc, mc, nw[0], ffn_w_in, ffn_w_out, l=l, s=0, mod0=0, rows_per_group=tm_c, tm=tm_c)

        wp = _pack_w_in(w_in[l], d)
        px = _inproj(xs, m, nw[1], wp, rows_per_group=seq, tm=tm_x)
        pc = _inproj(xc, mc, nw[1], wp, rows_per_group=tm_c, tm=tm_c)

        bias = _na_bias(na_rpb[l], seq // GRID_W)
        qn = mla_q_norm[l].reshape(1, MLA_Q_RANK)
        kvn = mla_kv_norm[l].reshape(1, MLA_KV_RANK)
        wuq = _pack_heads(mla_w_uq[l], N_HEADS_MLA, 2 * LANES)
        wukv = _pack_heads(mla_w_ukv[l], N_HEADS_MLA, 2 * LANES)
        subln = diff_subln[l].reshape(1, HEAD_DIM)
        wb = w_branch[l].astype(BF16)
        wb_na, wb_mla, wb_diff = wb[:NA_W], wb[NA_W:NA_W + MLA_W], wb[NA_W + MLA_W:]

        o_na = _na_attn(px, pc, bias, cols, batch=batch, seq=seq, ctx=n_ctx)
        o_mla = _mla_attn(px, pc, px, cos, sin, qn, kvn, wuq, wukv, cols,
                          batch=batch, n_q=seq, ctx=n_ctx, seq=seq, latent=True)
        o_diff = _diff_attn(px, pc, px, cos, sin, diff_lambda[l], subln, cols,
                            batch=batch, n_q=seq, ctx=n_ctx, seq=seq, latent=True, lambda_init=lambda_init)
        y = _merge(o_na, o_mla, o_diff, px, wb_na, wb_mla, wb_diff, cols, tm=tm_x)
        xs = _outproj(xs, y, m[:, 5:6], w_out, l=l, rows_per_group=seq, tm=tm_x)
        xs = _ffn(xs, m, nw[2], ffn_w_in, ffn_w_out, l=l, s=1, mod0=6, rows_per_group=seq, tm=tm_x)

        if not last:
            oc_na = _ctx_na_attn(pc, cols, batch=batch, ctx=n_ctx)
            oc_mla = _mla_attn(pc, pc, px, cos, sin, qn, kvn, wuq, wukv, cols,
                               batch=batch, n_q=n_ctx, ctx=n_ctx, seq=seq, latent=False)
            oc_diff = _diff_attn(pc, pc, px, cos, sin, diff_lambda[l], subln, cols,
                                 batch=batch, n_q=n_ctx, ctx=n_ctx, seq=seq, latent=False, lambda_init=lambda_init)
            yc = _merge(oc_na, oc_mla, oc_diff, pc, wb_na, wb_mla, wb_diff, cols, tm=tm_c)
            xc = _outproj(xc, yc, mc[:, 5:6], w_out, l=l, rows_per_group=tm_c, tm=tm_c)
            xc = _ffn(xc, mc, nw[2], ffn_w_in, ffn_w_out, l=l, s=1, mod0=6, rows_per_group=tm_c, tm=tm_c)

    return _final_norm(xs, final_norm.reshape(1, d), tm=tm_x).reshape(batch, seq, d)
```

```python
import functools
import math

import jax
import jax.numpy as jnp
from jax import lax
from jax.experimental import pallas as pl
from jax.experimental.pallas import tpu as pltpu

GRID_W = 64
HEAD_DIM = 128
N_HEADS_NA = 6
N_HEADS_MLA = 5
N_HEADS_DIFF = 5
NA_KH = 8
NA_KW = 16
MLA_Q_RANK = 768
MLA_KV_RANK = 512
MLA_NOPE_DIM = 128
MLA_ROPE_DIM = 64
MLA_V_DIM = 128
DIFF_QK_DIM = HEAD_DIM // 2
N_MOD = 9
ROPE_THETA = 10000.0
NORM_EPS = 1e-6
NEG_INF = -1e30
NA_W = N_HEADS_NA * HEAD_DIM
MLA_W = N_HEADS_MLA * MLA_V_DIM
DIFF_W = N_HEADS_DIFF * HEAD_DIM
NA_SCALE = HEAD_DIM ** -0.5
MLA_SCALE = (MLA_NOPE_DIM + MLA_ROPE_DIM) ** -0.5
DIFF_SCALE = DIFF_QK_DIM ** -0.5

LOG2E = 1.4426950408889634
LANES = 128
VMEM_LIMIT = 60 * 1024 * 1024
BF16 = jnp.bfloat16
F32 = jnp.float32


def _params(*sem):
    return pltpu.CompilerParams(dimension_semantics=sem, vmem_limit_bytes=VMEM_LIMIT)


def _pick(n, prefs):
    for p in prefs:
        if n % p == 0:
            return p
    return n


def _dot(a, b):
    return jnp.dot(a, b, preferred_element_type=F32)


def _dot_nt(a, b):
    return lax.dot_general(a, b, (((1,), (1,)), ((), ())), preferred_element_type=F32)


def _rms(x, g):
    return x * lax.rsqrt(jnp.mean(x * x, axis=-1, keepdims=True) + NORM_EPS) * g


def _silu(x):
    return x * (1.0 / (1.0 + jnp.exp(-x)))


def _sigmoid(x):
    return 1.0 / (1.0 + jnp.exp(-x))


def _swap16(x):
    lane = lax.broadcasted_iota(jnp.int32, x.shape, x.ndim - 1)
    down = pltpu.roll(x, 16, x.ndim - 1)
    up = pltpu.roll(x, LANES - 16, x.ndim - 1)
    return jnp.where((lane & 16) != 0, down, up)


def _rope(x, cos, sin):
    return x * cos + _swap16(x) * sin


NORM_ROWS = 16


def _modulated_norm(h_ref, x_ref, nw, shift, scale):
    gain = nw * (1.0 + scale)

    def group(i, carry):
        r0 = pl.multiple_of(i * NORM_ROWS, NORM_ROWS)
        x = x_ref[pl.ds(r0, NORM_ROWS), :]
        rstd = lax.rsqrt(jnp.mean(x * x, axis=-1, keepdims=True) + NORM_EPS)
        h_ref[pl.ds(r0, NORM_ROWS), :] = (x * rstd * gain + shift).astype(BF16)
        return carry

    lax.fori_loop(0, x_ref.shape[0] // NORM_ROWS, group, 0, unroll=4)


def _ada_kernel(c_ref, w_ref, b_ref, o_ref):
    a = _silu(c_ref[...]).astype(BF16)
    o_ref[...] = _dot(a, w_ref[...].astype(BF16)) + b_ref[...]


def _ada(cvec, w_ada, b_ada):
    depth, d, n = w_ada.shape
    tn = _pick(n, (1024, 512, 256, 128))
    return pl.pallas_call(
        _ada_kernel,
        grid=(depth, n // tn),
        in_specs=[
            pl.BlockSpec((8, d), lambda l, j: (0, 0)),
            pl.BlockSpec((None, d, tn), lambda l, j: (l, 0, j)),
            pl.BlockSpec((None, 1, tn), lambda l, j: (l, 0, j)),
        ],
        out_specs=pl.BlockSpec((None, 8, tn), lambda l, j: (l, 0, j)),
        out_shape=jax.ShapeDtypeStruct((depth, 8, n), F32),
        compiler_params=_params("parallel", "parallel"),
        name="ada",
    )(cvec, w_ada, b_ada.reshape(depth, 1, n))


def _ffn_in_kernel(x_ref, mod_ref, nw_ref, wa_ref, wb_ref, g_ref, h_ref, *, mod0):
    @pl.when(pl.program_id(1) == 0)
    def _():
        _modulated_norm(h_ref, x_ref, nw_ref[...], mod_ref[mod0:mod0 + 1, :], mod_ref[mod0 + 1:mod0 + 2, :])

    h = h_ref[...]
    a = _dot(h, wa_ref[...].astype(BF16))
    b = _dot(h, wb_ref[...].astype(BF16))
    g_ref[...] = (_silu(a) * b).astype(BF16)


def _ffn_out_kernel(x_ref, g_ref, mod_ref, w_ref, o_ref, *, mod0):
    gate = 0.5 * mod_ref[mod0 + 2:mod0 + 3, :]
    o_ref[...] = x_ref[...] + gate * _dot(g_ref[...], w_ref[...].astype(BF16))


def _ffn(x, mod, nw, w_in, w_out, *, l, s, mod0, rows_per_group, tm):
    n, d = x.shape
    f = w_out.shape[2]
    fc = _pick(f, (512, 256, 128))
    nf = f // fc
    tn = _pick(d, (256, 128))
    tiles_per_group = rows_per_group // tm
    g = pl.pallas_call(
        functools.partial(_ffn_in_kernel, mod0=mod0),
        grid=(n // tm, nf),
        in_specs=[
            pl.BlockSpec((tm, d), lambda i, j: (i, 0)),
            pl.BlockSpec((None, N_MOD, d), lambda i, j: (i // tiles_per_group, 0, 0)),
            pl.BlockSpec((1, d), lambda i, j: (0, 0)),
            pl.BlockSpec((None, None, d, fc), lambda i, j: (l, s, 0, j)),
            pl.BlockSpec((None, None, d, fc), lambda i, j: (l, s, 0, nf + j)),
        ],
        out_specs=pl.BlockSpec((tm, fc), lambda i, j: (i, j)),
        out_shape=jax.ShapeDtypeStruct((n, f), BF16),
        scratch_shapes=[pltpu.VMEM((tm, d), BF16)],
        compiler_params=_params("parallel", "arbitrary"),
        name="ffn_in",
    )(x, mod, nw, w_in, w_in)
    return pl.pallas_call(
        functools.partial(_ffn_out_kernel, mod0=mod0),
        grid=(n // tm, d // tn),
        in_specs=[
            pl.BlockSpec((tm, tn), lambda i, j: (i, j)),
            pl.BlockSpec((tm, f), lambda i, j: (i, 0)),
            pl.BlockSpec((None, N_MOD, tn), lambda i, j: (i // tiles_per_group, 0, j)),
            pl.BlockSpec((None, None, f, tn), lambda i, j: (l, s, 0, j)),
        ],
        out_specs=pl.BlockSpec((tm, tn), lambda i, j: (i, j)),
        out_shape=jax.ShapeDtypeStruct((n, d), F32),
        compiler_params=_params("parallel", "parallel"),
        name="ffn_out",
    )(x, g, mod, w_out)


def _final_norm_kernel(x_ref, g_ref, o_ref):
    o_ref[...] = _rms(x_ref[...], g_ref[...])


def _final_norm(x, g, *, tm):
    n, d = x.shape
    return pl.pallas_call(
        _final_norm_kernel,
        grid=(n // tm,),
        in_specs=[pl.BlockSpec((tm, d), lambda i: (i, 0)), pl.BlockSpec((1, d), lambda i: (0, 0))],
        out_specs=pl.BlockSpec((tm, d), lambda i: (i, 0)),
        out_shape=jax.ShapeDtypeStruct((n, d), F32),
        compiler_params=_params("parallel"),
        name="final_norm",
    )(x, g)


def _inproj_kernel(x_ref, mod_ref, nw_ref, w_ref, o_ref, h_ref):
    @pl.when(pl.program_id(1) == 0)
    def _():
        _modulated_norm(h_ref, x_ref, nw_ref[...], mod_ref[3:4, :], mod_ref[4:5, :])

    o_ref[...] = _dot(h_ref[...], w_ref[...]).astype(BF16)


def _inproj(x, mod, nw, w, *, l, rows_per_group, tm):
    n, d = x.shape
    nw_cols = w.shape[2]
    tn = _pick(nw_cols, (512, 256, 128))
    tiles_per_group = rows_per_group // tm
    return pl.pallas_call(
        _inproj_kernel,
        grid=(n // tm, nw_cols // tn),
        in_specs=[
            pl.BlockSpec((tm, d), lambda i, j: (i, 0)),
            pl.BlockSpec((None, N_MOD, d), lambda i, j: (i // tiles_per_group, 0, 0)),
            pl.BlockSpec((1, d), lambda i, j: (0, 0)),
            pl.BlockSpec((None, d, tn), lambda i, j: (l, 0, j)),
        ],
        out_specs=pl.BlockSpec((tm, tn), lambda i, j: (i, j)),
        out_shape=jax.ShapeDtypeStruct((n, nw_cols), BF16),
        scratch_shapes=[pltpu.VMEM((tm, d), BF16)],
        compiler_params=_params("parallel", "arbitrary"),
        name="inproj",
    )(x, mod, nw, w)


class _Cols:
    def __init__(self, d):
        self.na_q = 0
        self.na_k = NA_W
        self.na_v = 2 * NA_W
        self.cq = 3 * NA_W
        self.ckv = self.cq + MLA_Q_RANK
        self.gates = self.ckv + MLA_KV_RANK
        self.dq = self.gates + 3 * d
        self.dk = self.dq + DIFF_W
        self.dv = self.dk + DIFF_W
        self.kpe = self.dv + DIFF_W
        self.total = self.kpe + LANES


def _pack_w_in_kernel(w_ref, o_ref):
    w = w_ref[...]
    o = 3 * NA_W + MLA_Q_RANK + MLA_KV_RANK
    o2 = o + MLA_ROPE_DIM
    o3 = o2 + 3 * DIFF_W
    pad = jnp.zeros((w.shape[0], LANES - MLA_ROPE_DIM), w.dtype)
    o_ref[...] = jnp.concatenate([w[:, :o], w[:, o3:], w[:, o2:o3], w[:, o:o2], pad], axis=1).astype(BF16)


def _pack_w_in(w_in):
    depth, d, n = w_in.shape
    tr = _pick(d, (64, 32, 16))
    n_out = n + LANES - MLA_ROPE_DIM
    return pl.pallas_call(
        _pack_w_in_kernel,
        grid=(depth, d // tr),
        in_specs=[pl.BlockSpec((None, tr, n), lambda l, i: (l, i, 0))],
        out_specs=pl.BlockSpec((None, tr, n_out), lambda l, i: (l, i, 0)),
        out_shape=jax.ShapeDtypeStruct((depth, d, n_out), BF16),
        compiler_params=_params("parallel", "parallel"),
        name="pack_w_in",
    )(w_in)


def _softmax_pv(parts):
    mx = parts[0][0].max(axis=-1, keepdims=True)
    for s, _ in parts[1:]:
        mx = jnp.maximum(mx, s.max(axis=-1, keepdims=True))
    acc = None
    den = None
    for s, v in parts:
        e = jnp.exp(s - mx)
        den_i = e.sum(axis=-1, keepdims=True)
        o_i = _dot(e.astype(BF16), v)
        acc = o_i if acc is None else acc + o_i
        den = den_i if den is None else den + den_i
    return acc * (1.0 / den)


NA_ROWS = 4
NA_UNION = NA_ROWS + NA_KH


def _na_union_start(r, n_rows):
    if isinstance(r, int):
        return min(min(max(r - NA_KH // 2, 0), n_rows - NA_KH), n_rows - NA_UNION)
    return jnp.minimum(jnp.clip(r - NA_KH // 2, 0, n_rows - NA_KH), n_rows - NA_UNION)


def _na_kernel(q_ref, k_ref, v_ref, kc_ref, vc_ref, bias_ref, o_ref, *, n_rows):
    u0 = _na_union_start(pl.program_id(1) * NA_ROWS, n_rows)
    k0 = pl.multiple_of(u0 * GRID_W, GRID_W)
    n_loc = NA_UNION * GRID_W
    c = NA_SCALE * LOG2E
    cs = [slice(h * HEAD_DIM, (h + 1) * HEAD_DIM) for h in range(N_HEADS_NA)]
    s_loc = [_dot_nt(q_ref[:, cs[h]], k_ref[pl.ds(k0, n_loc), cs[h]]) * c + bias_ref[h] for h in range(N_HEADS_NA)]
    s_ctx = [_dot_nt(q_ref[:, cs[h]], kc_ref[:, cs[h]]) * c for h in range(N_HEADS_NA)]
    for h in range(N_HEADS_NA):
        m = jnp.maximum(s_loc[h].max(axis=-1, keepdims=True), s_ctx[h].max(axis=-1, keepdims=True))
        e_loc = jnp.exp2(s_loc[h] - m)
        e_ctx = jnp.exp2(s_ctx[h] - m)
        den = e_loc.sum(axis=-1, keepdims=True) + e_ctx.sum(axis=-1, keepdims=True)
        o = _dot(e_loc.astype(BF16), v_ref[pl.ds(k0, n_loc), cs[h]]) + _dot(e_ctx.astype(BF16), vc_ref[:, cs[h]])
        o_ref[:, cs[h]] = (o * (1.0 / den)).astype(BF16)


def _na_bias(rpb, n_rows):
    assert n_rows >= NA_UNION and n_rows % NA_ROWS == 0
    h, n_ro, _ = rpb.shape
    w = GRID_W
    col = jnp.arange(w)
    c0 = jnp.clip(col - NA_KW // 2, 0, w - NA_KW)
    col_ok = (col[None, :] >= c0[:, None]) & (col[None, :] < c0[:, None] + NA_KW)
    edge = w - NA_KW
    rpb = rpb.astype(F32)
    text = jnp.concatenate([jnp.broadcast_to(rpb[..., :1], (h, n_ro, edge)), rpb,
                            jnp.broadcast_to(rpb[..., -1:], (h, n_ro, edge)),
                            jnp.zeros((h, n_ro, 1), F32)], axis=-1)
    skew = jnp.tile(text, (1, 1, w))[..., :w * (2 * w - 1)].reshape(h, n_ro, w, 2 * w - 1)
    t = jnp.where(col_ok[None, None], skew[..., w - 1:], NEG_INF) * LOG2E
    masked = jnp.full((h, w, w), NEG_INF * LOG2E, F32)
    kinds = []
    for r in (0, NA_ROWS, n_rows - NA_ROWS):
        u0 = _na_union_start(r, n_rows)
        q_rows = []
        for qr in range(r, r + NA_ROWS):
            r0 = min(max(qr - NA_KH // 2, 0), n_rows - NA_KH)
            tiles = [t[:, kr - qr + NA_KH - 1] if r0 <= kr < r0 + NA_KH else masked
                     for kr in range(u0, u0 + NA_UNION)]
            q_rows.append(jnp.concatenate(tiles, axis=-1))
        kinds.append(jnp.concatenate(q_rows, axis=1))
    return jnp.stack(kinds, axis=0)


def _na_attn(px, pc, bias, cols, *, batch, seq, ctx):
    n_rows = seq // GRID_W
    steps = n_rows // NA_ROWS
    tq = NA_ROWS * GRID_W
    qb, kb, vb = cols.na_q // NA_W, cols.na_k // NA_W, cols.na_v // NA_W
    kind = lambda i: jnp.where(i == 0, 0, jnp.where(i == steps - 1, 2, 1))
    return pl.pallas_call(
        functools.partial(_na_kernel, n_rows=n_rows),
        grid=(batch, steps),
        in_specs=[
            pl.BlockSpec((tq, NA_W), lambda b, i: (b * steps + i, qb)),
            pl.BlockSpec((seq, NA_W), lambda b, i: (b, kb)),
            pl.BlockSpec((seq, NA_W), lambda b, i: (b, vb)),
            pl.BlockSpec((ctx, NA_W), lambda b, i: (b, kb)),
            pl.BlockSpec((ctx, NA_W), lambda b, i: (b, vb)),
            pl.BlockSpec((None,) + bias.shape[1:], lambda b, i: (kind(i), 0, 0, 0)),
        ],
        out_specs=pl.BlockSpec((tq, NA_W), lambda b, i: (b * steps + i, 0)),
        out_shape=jax.ShapeDtypeStruct((batch * seq, NA_W), BF16),
        compiler_params=_params("parallel", "arbitrary"),
        name="na_attn",
    )(px, px, px, pc, pc, bias)


def _ctx_na_kernel(q_ref, k_ref, v_ref, o_ref):
    for h in range(N_HEADS_NA):
        cs = slice(h * HEAD_DIM, (h + 1) * HEAD_DIM)
        s = _dot_nt(q_ref[:, cs], k_ref[:, cs]) * NA_SCALE
        o_ref[:, cs] = _softmax_pv([(s, v_ref[:, cs])]).astype(BF16)


def _ctx_na_attn(pc, cols, *, batch, ctx):
    qb, kb, vb = cols.na_q // NA_W, cols.na_k // NA_W, cols.na_v // NA_W
    return pl.pallas_call(
        _ctx_na_kernel,
        grid=(batch,),
        in_specs=[
            pl.BlockSpec((ctx, NA_W), lambda b: (b, qb)),
            pl.BlockSpec((ctx, NA_W), lambda b: (b, kb)),
            pl.BlockSpec((ctx, NA_W), lambda b: (b, vb)),
        ],
        out_specs=pl.BlockSpec((ctx, NA_W), lambda b: (b, 0)),
        out_shape=jax.ShapeDtypeStruct((batch * ctx, NA_W), BF16),
        compiler_params=_params("parallel"),
        name="ctx_na_attn",
    )(pc, pc, pc)


def _attn_pipeline_step(qs, k_scr, v_scr, park, parked):
    n_sets = park[0].shape[0] if park is not None else parked[0].shape[0]
    tq = park[0].shape[1] if park is not None else parked[0].shape[1]
    n_keys = k_scr.shape[0]
    ch = _pick(n_keys, (2 * LANES, LANES))
    groups = ch // LANES
    run = [None] * n_sets
    acc = [None] * n_sets
    for c in range(n_keys // ch):
        ks = slice(c * ch, (c + 1) * ch)
        for t in range(n_sets):
            if park is not None:
                s = _dot_nt(qs[t], k_scr[ks, :])
                park[0][t, :, ks] = s
                m = s[:, :LANES]
                for g in range(1, groups):
                    m = jnp.maximum(m, s[:, g * LANES:(g + 1) * LANES])
                run[t] = m if run[t] is None else jnp.maximum(run[t], m)
            if parked is not None:
                e = jnp.exp2(parked[0][t, :, ks] - jnp.concatenate([parked[1][t]] * groups, axis=1))
                pv = _dot(e.astype(BF16), v_scr[ks, :])
                acc[t] = pv if acc[t] is None else acc[t] + pv
    if park is not None:
        for t in range(n_sets):
            park[1][t] = jnp.broadcast_to(run[t].max(axis=-1, keepdims=True), (tq, LANES))
    return acc


def _attn_pipeline(step, last, bufs, body):
    @pl.when(step == 0)
    def _():
        body(bufs[0], None)

    if last > 1:
        @pl.when((step > 0) & (step < last) & (step % 2 == 0))
        def _():
            body(bufs[0], bufs[1])

        @pl.when((step < last) & (step % 2 == 1))
        def _():
            body(bufs[1], bufs[0])

    @pl.when(step == last)
    def _():
        body(None, bufs[(last - 1) % 2])


def _mla_q_kernel(*refs, latent):
    if latent:
        cq_ref, cos_ref, sin_ref, qn_ref, wuq_ref, o_ref, n_scr = refs
    else:
        cq_ref, qn_ref, wuq_ref, o_ref, n_scr = refs

    @pl.when(pl.program_id(1) == 0)
    def _():
        n_scr[...] = _rms(cq_ref[...].astype(F32), qn_ref[...]).astype(BF16)

    q = _dot(n_scr[...], wuq_ref[...])
    q_pe = q[:, MLA_NOPE_DIM:]
    if latent:
        q_pe = _rope(q_pe, cos_ref[...], sin_ref[...])
    o_ref[...] = (jnp.concatenate([q[:, :MLA_NOPE_DIM], q_pe], axis=1) * (MLA_SCALE * LOG2E)).astype(BF16)


def _mla_q(pq, cos, sin, qn, wuq, cols, *, n_q, latent):
    n = pq.shape[0]
    tm = _pick(n_q, (512, 256, 128))
    tiles = n_q // tm
    cqb = cols.cq // MLA_Q_RANK
    in_specs = [pl.BlockSpec((tm, MLA_Q_RANK), lambda i, h: (i, cqb))]
    args = [pq]
    if latent:
        in_specs += [pl.BlockSpec((tm, LANES), lambda i, h: (i % tiles, 0))] * 2
        args += [cos, sin]
    in_specs += [pl.BlockSpec((1, MLA_Q_RANK), lambda i, h: (0, 0)),
                 pl.BlockSpec((None, MLA_Q_RANK, 2 * LANES), lambda i, h: (h, 0, 0))]
    args += [qn, wuq]
    return pl.pallas_call(
        functools.partial(_mla_q_kernel, latent=latent),
        grid=(n // tm, N_HEADS_MLA),
        in_specs=in_specs,
        out_specs=pl.BlockSpec((tm, 2 * LANES), lambda i, h: (i, h)),
        out_shape=jax.ShapeDtypeStruct((n, N_HEADS_MLA * 2 * LANES), BF16),
        scratch_shapes=[pltpu.VMEM((tm, MLA_Q_RANK), BF16)],
        compiler_params=_params("parallel", "arbitrary"),
        name="mla_q" if latent else "ctx_mla_q",
    )(*args)


def _mla_kernel(*refs, ctx, seq, latent, last):
    if latent:
        (q_ref, ckvc_ref, kpec_ref, ckvl_ref, kpel_ref, cosk_ref, sink_ref,
         kvn_ref, wukv_ref, o_ref, k_scr, v_scr, s0, m0, s1, m1) = refs
    else:
        q_ref, ckvc_ref, kpec_ref, kvn_ref, wukv_ref, o_ref, k_scr, v_scr, s0, m0, s1, m1 = refs
    step = pl.program_id(2)

    @pl.when(step == 0)
    def _():
        w = wukv_ref[...]

        def fill(row0, n, ckv_ref, kpe):
            kv = _dot(_rms(ckv_ref[...].astype(F32), kvn_ref[...]).astype(BF16), w)
            k_scr[row0:row0 + n, 0:MLA_NOPE_DIM] = kv[:, :MLA_NOPE_DIM].astype(BF16)
            k_scr[row0:row0 + n, MLA_NOPE_DIM:] = kpe.astype(BF16)
            v_scr[row0:row0 + n, 0:MLA_V_DIM] = kv[:, MLA_NOPE_DIM:].astype(BF16)
            v_scr[row0:row0 + n, MLA_V_DIM:] = jnp.ones((n, LANES), BF16)

        fill(0, ctx, ckvc_ref, kpec_ref[...])
        if latent:
            fill(ctx, seq, ckvl_ref, _rope(kpel_ref[...].astype(F32), cosk_ref[...], sink_ref[...]))

    def body(park, parked):
        qs = [q_ref[...]] if park is not None else None
        acc = _attn_pipeline_step(qs, k_scr, v_scr, park, parked)
        if parked is not None:
            o_ref[...] = (acc[0][:, :MLA_V_DIM] * (1.0 / acc[0][:, MLA_V_DIM:])).astype(BF16)

    _attn_pipeline(step, last, ((s0, m0), (s1, m1)), body)


def _mla_attn(q, pc, px, cos, sin, kvn, wukv, cols, *, batch, n_q, ctx, seq, latent):
    tq = _pick(n_q, (256, 128))
    nq = n_q // tq
    ckvb, kpeb = cols.ckv // MLA_KV_RANK, cols.kpe // LANES
    n_keys = ctx + (seq if latent else 0)
    q_tile = lambda i: jnp.minimum(i, nq - 1)
    o_tile = lambda i: jnp.maximum(i - 1, 0)
    in_specs = [
        pl.BlockSpec((tq, 2 * LANES), lambda b, h, i: (b * nq + q_tile(i), h)),
        pl.BlockSpec((ctx, MLA_KV_RANK), lambda b, h, i: (b, ckvb)),
        pl.BlockSpec((ctx, LANES), lambda b, h, i: (b, kpeb)),
    ]
    args = [q, pc, pc]
    if latent:
        in_specs += [
            pl.BlockSpec((seq, MLA_KV_RANK), lambda b, h, i: (b, ckvb)),
            pl.BlockSpec((seq, LANES), lambda b, h, i: (b, kpeb)),
            pl.BlockSpec((seq, LANES), lambda b, h, i: (0, 0)),
            pl.BlockSpec((seq, LANES), lambda b, h, i: (0, 0)),
        ]
        args += [px, px, cos, sin]
    in_specs += [
        pl.BlockSpec((1, MLA_KV_RANK), lambda b, h, i: (0, 0)),
        pl.BlockSpec((None, MLA_KV_RANK, 2 * LANES), lambda b, h, i: (h, 0, 0)),
    ]
    args += [kvn, wukv]
    return pl.pallas_call(
        functools.partial(_mla_kernel, ctx=ctx, seq=seq, latent=latent, last=nq),
        grid=(batch, N_HEADS_MLA, nq + 1),
        in_specs=in_specs,
        out_specs=pl.BlockSpec((tq, MLA_V_DIM), lambda b, h, i: (b * nq + o_tile(i), h)),
        out_shape=jax.ShapeDtypeStruct((batch * n_q, MLA_W), BF16),
        scratch_shapes=[pltpu.VMEM((n_keys, 2 * LANES), BF16), pltpu.VMEM((n_keys, 2 * LANES), BF16)]
        + 2 * [pltpu.VMEM((1, tq, n_keys), F32), pltpu.VMEM((1, tq, LANES), F32)],
        compiler_params=_params("parallel", "parallel", "arbitrary"),
        name="mla_attn" if latent else "ctx_mla_attn",
    )(*args)


def _diff_kernel(*refs, ctx, seq, latent, lambda_init, last):
    if latent:
        (q_ref, kc_ref, vc_ref, kl_ref, vl_ref, cosq_ref, sinq_ref, cosk_ref, sink_ref,
         lam_ref, sub_ref, o_ref, k_scr, v_scr, s0, m0, s1, m1) = refs
    else:
        q_ref, kc_ref, vc_ref, lam_ref, sub_ref, o_ref, k_scr, v_scr, s0, m0, s1, m1 = refs
    step = pl.program_id(2)

    @pl.when(step == 0)
    def _():
        k_scr[0:ctx, :] = kc_ref[...]
        v_scr[0:ctx, 0:HEAD_DIM] = vc_ref[...]
        if latent:
            k_scr[ctx:ctx + seq, :] = _rope(kl_ref[...].astype(F32), cosk_ref[...], sink_ref[...]).astype(BF16)
            v_scr[ctx:ctx + seq, 0:HEAD_DIM] = vl_ref[...]
        v_scr[:, HEAD_DIM:] = jnp.ones((v_scr.shape[0], LANES), BF16)

    def body(park, parked):
        qs = None
        if park is not None:
            q = q_ref[...].astype(F32)
            if latent:
                q = _rope(q, cosq_ref[...], sinq_ref[...])
            q = q * (DIFF_SCALE * LOG2E)
            lane = lax.broadcasted_iota(jnp.int32, q.shape, 1)
            qs = [jnp.where(lane < DIFF_QK_DIM, q, 0.0).astype(BF16),
                  jnp.where(lane >= DIFF_QK_DIM, q, 0.0).astype(BF16)]
        acc = _attn_pipeline_step(qs, k_scr, v_scr, park, parked)
        if parked is not None:
            a1, a2 = acc
            dl = lam_ref[...]
            lam = (jnp.exp(jnp.sum(dl[0:1] * dl[1:2], axis=-1, keepdims=True))
                   - jnp.exp(jnp.sum(dl[2:3] * dl[3:4], axis=-1, keepdims=True)) + lambda_init)
            o = a1[:, :HEAD_DIM] * (1.0 / a1[:, HEAD_DIM:]) - a2[:, :HEAD_DIM] * (lam / a2[:, HEAD_DIM:])
            o_ref[...] = (_rms(o, sub_ref[...]) * (1.0 - lambda_init)).astype(BF16)

    _attn_pipeline(step, last, ((s0, m0), (s1, m1)), body)


def _diff_attn(pq, pc, px, cos, sin, dlam, subln, cols, *, batch, n_q, ctx, seq, latent, lambda_init):
    tq = _pick(n_q, (256, 128))
    nq = n_q // tq
    qb, kb, vb = cols.dq // HEAD_DIM, cols.dk // HEAD_DIM, cols.dv // HEAD_DIM
    n_keys = ctx + (seq if latent else 0)
    q_tile = lambda i: jnp.minimum(i, nq - 1)
    o_tile = lambda i: jnp.maximum(i - 1, 0)
    in_specs = [
        pl.BlockSpec((tq, HEAD_DIM), lambda b, h, i: (b * nq + q_tile(i), qb + h)),
        pl.BlockSpec((ctx, HEAD_DIM), lambda b, h, i: (b, kb + h)),
        pl.BlockSpec((ctx, HEAD_DIM), lambda b, h, i: (b, vb + h)),
    ]
    args = [pq, pc, pc]
    if latent:
        in_specs += [
            pl.BlockSpec((seq, HEAD_DIM), lambda b, h, i: (b, kb + h)),
            pl.BlockSpec((seq, HEAD_DIM), lambda b, h, i: (b, vb + h)),
            pl.BlockSpec((tq, LANES), lambda b, h, i: (q_tile(i), 0)),
            pl.BlockSpec((tq, LANES), lambda b, h, i: (q_tile(i), 0)),
            pl.BlockSpec((seq, LANES), lambda b, h, i: (0, 0)),
            pl.BlockSpec((seq, LANES), lambda b, h, i: (0, 0)),
        ]
        args += [px, px, cos, sin, cos, sin]
    in_specs += [
        pl.BlockSpec((4, DIFF_QK_DIM), lambda b, h, i: (0, 0)),
        pl.BlockSpec((1, HEAD_DIM), lambda b, h, i: (0, 0)),
    ]
    args += [dlam, subln]
    return pl.pallas_call(
        functools.partial(_diff_kernel, ctx=ctx, seq=seq, latent=latent, lambda_init=lambda_init, last=nq),
        grid=(batch, N_HEADS_DIFF, nq + 1),
        in_specs=in_specs,
        out_specs=pl.BlockSpec((tq, HEAD_DIM), lambda b, h, i: (b * nq + o_tile(i), h)),
        out_shape=jax.ShapeDtypeStruct((batch * n_q, DIFF_W), BF16),
        scratch_shapes=[pltpu.VMEM((n_keys, HEAD_DIM), BF16), pltpu.VMEM((n_keys, 2 * LANES), BF16)]
        + 2 * [pltpu.VMEM((2, tq, n_keys), F32), pltpu.VMEM((2, tq, LANES), F32)],
        compiler_params=_params("parallel", "parallel", "arbitrary"),
        name="diff_attn" if latent else "ctx_diff_attn",
    )(*args)


def _merge_kernel(oa_ref, ob_ref, od_ref, ga_ref, gb_ref, gd_ref, wa_ref, wb_ref, wd_ref, y_ref):
    y = (_sigmoid(ga_ref[...].astype(F32)) * _dot(oa_ref[...], wa_ref[...])
         + _sigmoid(gb_ref[...].astype(F32)) * _dot(ob_ref[...], wb_ref[...])
         + _sigmoid(gd_ref[...].astype(F32)) * _dot(od_ref[...], wd_ref[...]))
    y_ref[...] = y.astype(BF16)


def _merge(o_na, o_mla, o_diff, p, wb_na, wb_mla, wb_diff, cols, *, tm):
    n = o_na.shape[0]
    d = wb_na.shape[1]
    tn = _pick(d, (512, 256, 128))
    g0 = cols.gates // tn
    gs = d // tn
    return pl.pallas_call(
        _merge_kernel,
        grid=(n // tm, d // tn),
        in_specs=[
            pl.BlockSpec((tm, NA_W), lambda i, j: (i, 0)),
            pl.BlockSpec((tm, MLA_W), lambda i, j: (i, 0)),
            pl.BlockSpec((tm, DIFF_W), lambda i, j: (i, 0)),
            pl.BlockSpec((tm, tn), lambda i, j: (i, g0 + j)),
            pl.BlockSpec((tm, tn), lambda i, j: (i, g0 + gs + j)),
            pl.BlockSpec((tm, tn), lambda i, j: (i, g0 + 2 * gs + j)),
            pl.BlockSpec((NA_W, tn), lambda i, j: (0, j)),
            pl.BlockSpec((MLA_W, tn), lambda i, j: (0, j)),
            pl.BlockSpec((DIFF_W, tn), lambda i, j: (0, j)),
        ],
        out_specs=pl.BlockSpec((tm, tn), lambda i, j: (i, j)),
        out_shape=jax.ShapeDtypeStruct((n, d), BF16),
        compiler_params=_params("parallel", "parallel"),
        name="merge",
    )(o_na, o_mla, o_diff, p, p, p, wb_na, wb_mla, wb_diff)


def _outproj_kernel(x_ref, y_ref, mod_ref, w_ref, o_ref):
    o_ref[...] = x_ref[...] + mod_ref[...] * _dot(y_ref[...], w_ref[...].astype(BF16))


def _outproj(x, y, gate, w_out, *, l, rows_per_group, tm):
    n, d = x.shape
    tn = _pick(d, (512, 256, 128))
    tiles_per_group = rows_per_group // tm
    return pl.pallas_call(
        _outproj_kernel,
        grid=(n // tm, d // tn),
        in_specs=[
            pl.BlockSpec((tm, tn), lambda i, j: (i, j)),
            pl.BlockSpec((tm, d), lambda i, j: (i, 0)),
            pl.BlockSpec((None, 1, tn), lambda i, j: (i // tiles_per_group, 0, j)),
            pl.BlockSpec((None, d, tn), lambda i, j: (l, 0, j)),
        ],
        out_specs=pl.BlockSpec((tm, tn), lambda i, j: (i, j)),
        out_shape=jax.ShapeDtypeStruct((n, d), F32),
        compiler_params=_params("parallel", "parallel"),
        name="outproj",
    )(x, y, gate, w_out)


def _rope_tables(seq):
    quarter = MLA_ROPE_DIM // 4
    freqs = ROPE_THETA ** (-jnp.arange(quarter, dtype=F32) / quarter)
    t = jnp.arange(seq)
    rows, cols = t // GRID_W, t % GRID_W

    def unit(pos):
        ang = pos.astype(F32)[:, None] * freqs
        c, s = jnp.cos(ang), jnp.sin(ang)
        return jnp.concatenate([c, c], axis=1), jnp.concatenate([-s, s], axis=1)

    cr, sr = unit(rows)
    cc, sc = unit(cols)
    cos = jnp.concatenate([cr, cc], axis=1)
    sin = jnp.concatenate([sr, sc], axis=1)
    reps = LANES // MLA_ROPE_DIM
    return jnp.tile(cos, (1, reps)), jnp.tile(sin, (1, reps))


def _pack_heads(w, n_heads, width):
    k = w.shape[0]
    per = w.shape[1] // n_heads
    w = jnp.moveaxis(w.reshape(k, n_heads, per), 1, 0)
    if per < width:
        w = jnp.concatenate([w, jnp.zeros((n_heads, k, width - per), w.dtype)], axis=2)
    return w.astype(BF16)


def kernel(x, c, ctx, c_ctx, w_ada, b_ada, norm_w, ffn_w_in, ffn_w_out, w_in, na_rpb, mla_q_norm, mla_kv_norm,
           mla_w_uq, mla_w_ukv, diff_lambda, diff_subln, w_branch, w_out, final_norm):
    batch, seq, d = x.shape
    n_ctx = ctx.shape[1]
    depth = w_ada.shape[0]
    cols = _Cols(d)
    tm_x = _pick(seq, (1024, 512, 256, 128))
    tm_c = batch * n_ctx

    xs = x.reshape(batch * seq, d)
    xc = ctx.reshape(batch * n_ctx, d)
    cvec = jnp.concatenate([c, c_ctx[None, :], jnp.zeros((8 - batch - 1, d), F32)], axis=0)
    mods = _ada(cvec, w_ada, b_ada).reshape(depth, 8, N_MOD, d)
    cos, sin = _rope_tables(seq)
    wp = _pack_w_in(w_in)

    for l in range(depth):
        last = l == depth - 1
        lambda_init = 0.8 - 0.6 * math.exp(-0.3 * l)
        m = mods[l, :batch]
        mc = mods[l, batch:batch + 1]
        nw = norm_w[l].reshape(3, 1, d)

        xs = _ffn(xs, m, nw[0], ffn_w_in, ffn_w_out, l=l, s=0, mod0=0, rows_per_group=seq, tm=tm_x)
        xc = _ffn(xc, mc, nw[0], ffn_w_in, ffn_w_out, l=l, s=0, mod0=0, rows_per_group=tm_c, tm=tm_c)

        px = _inproj(xs, m, nw[1], wp, l=l, rows_per_group=seq, tm=tm_x)
        pc = _inproj(xc, mc, nw[1], wp, l=l, rows_per_group=tm_c, tm=tm_c)

        bias = _na_bias(na_rpb[l], seq // GRID_W)
        qn = mla_q_norm[l].reshape(1, MLA_Q_RANK)
        kvn = mla_kv_norm[l].reshape(1, MLA_KV_RANK)
        wuq = _pack_heads(mla_w_uq[l], N_HEADS_MLA, 2 * LANES)
        wukv = _pack_heads(mla_w_ukv[l], N_HEADS_MLA, 2 * LANES)
        subln = diff_subln[l].reshape(1, HEAD_DIM)
        wb = w_branch[l].astype(BF16)
        wb_na, wb_mla, wb_diff = wb[:NA_W], wb[NA_W:NA_W + MLA_W], wb[NA_W + MLA_W:]

        o_na = _na_attn(px, pc, bias, cols, batch=batch, seq=seq, ctx=n_ctx)
        q_mla = _mla_q(px, cos, sin, qn, wuq, cols, n_q=seq, latent=True)
        o_mla = _mla_attn(q_mla, pc, px, cos, sin, kvn, wukv, cols,
                          batch=batch, n_q=seq, ctx=n_ctx, seq=seq, latent=True)
        o_diff = _diff_attn(px, pc, px, cos, sin, diff_lambda[l], subln, cols,
                            batch=batch, n_q=seq, ctx=n_ctx, seq=seq, latent=True, lambda_init=lambda_init)
        y = _merge(o_na, o_mla, o_diff, px, wb_na, wb_mla, wb_diff, cols, tm=tm_x)
        xs = _outproj(xs, y, m[:, 5:6], w_out, l=l, rows_per_group=seq, tm=tm_x)
        xs = _ffn(xs, m, nw[2], ffn_w_in, ffn_w_out, l=l, s=1, mod0=6, rows_per_group=seq, tm=tm_x)

        if not last:
            oc_na = _ctx_na_attn(pc, cols, batch=batch, ctx=n_ctx)
            qc_mla = _mla_q(pc, cos, sin, qn, wuq, cols, n_q=n_ctx, latent=False)
            oc_mla = _mla_attn(qc_mla, pc, px, cos, sin, kvn, wukv, cols,
                               batch=batch, n_q=n_ctx, ctx=n_ctx, seq=seq, latent=False)
            oc_diff = _diff_attn(pc, pc, px, cos, sin, diff_lambda[l], subln, cols,
                                 batch=batch, n_q=n_ctx, ctx=n_ctx, seq=seq, latent=False, lambda_init=lambda_init)
            yc = _merge(oc_na, oc_mla, oc_diff, pc, wb_na, wb_mla, wb_diff, cols, tm=tm_c)
            xc = _outproj(xc, yc, mc[:, 5:6], w_out, l=l, rows_per_group=tm_c, tm=tm_c)
            xc = _ffn(xc, mc, nw[2], ffn_w_in, ffn_w_out, l=l, s=1, mod0=6, rows_per_group=tm_c, tm=tm_c)

    return _final_norm(xs, final_norm.reshape(1, d), tm=tm_x).reshape(batch, seq, d)
```

```python
import functools
import math

import jax
import jax.numpy as jnp
from jax import lax
from jax.experimental import pallas as pl
from jax.experimental.pallas import tpu as pltpu

GRID_W = 64
HEAD_DIM = 128
N_HEADS_NA = 6
N_HEADS_MLA = 5
N_HEADS_DIFF = 5
NA_KH = 8
NA_KW = 16
MLA_Q_RANK = 768
MLA_KV_RANK = 512
MLA_NOPE_DIM = 128
MLA_ROPE_DIM = 64
MLA_V_DIM = 128
DIFF_QK_DIM = HEAD_DIM // 2
N_MOD = 9
ROPE_THETA = 10000.0
NORM_EPS = 1e-6
NEG_INF = -1e30
NA_W = N_HEADS_NA * HEAD_DIM
MLA_W = N_HEADS_MLA * MLA_V_DIM
DIFF_W = N_HEADS_DIFF * HEAD_DIM
NA_SCALE = HEAD_DIM ** -0.5
MLA_SCALE = (MLA_NOPE_DIM + MLA_ROPE_DIM) ** -0.5
DIFF_SCALE = DIFF_QK_DIM ** -0.5

LOG2E = 1.4426950408889634
LANES = 128
VMEM_LIMIT = 60 * 1024 * 1024
BF16 = jnp.bfloat16
F32 = jnp.float32


def _params(*sem):
    return pltpu.CompilerParams(dimension_semantics=sem, vmem_limit_bytes=VMEM_LIMIT)


def _pick(n, prefs):
    for p in prefs:
        if n % p == 0:
            return p
    return n


def _dot(a, b):
    return jnp.dot(a, b, preferred_element_type=F32)


def _dot_nt(a, b):
    return lax.dot_general(a, b, (((1,), (1,)), ((), ())), preferred_element_type=F32)


def _rms(x, g):
    return x * lax.rsqrt(jnp.mean(x * x, axis=-1, keepdims=True) + NORM_EPS) * g


def _silu(x):
    return x * (1.0 / (1.0 + jnp.exp(-x)))


def _sigmoid(x):
    return 1.0 / (1.0 + jnp.exp(-x))


def _swap16(x):
    lane = lax.broadcasted_iota(jnp.int32, x.shape, x.ndim - 1)
    down = pltpu.roll(x, 16, x.ndim - 1)
    up = pltpu.roll(x, LANES - 16, x.ndim - 1)
    return jnp.where((lane & 16) != 0, down, up)


def _rope(x, cos, sin):
    return x * cos + _swap16(x) * sin


NORM_ROWS = 16


def _modulated_norm(h_ref, x_ref, nw, shift, scale):
    gain = nw * (1.0 + scale)

    def group(i, carry):
        r0 = pl.multiple_of(i * NORM_ROWS, NORM_ROWS)
        x = x_ref[pl.ds(r0, NORM_ROWS), :]
        rstd = lax.rsqrt(jnp.mean(x * x, axis=-1, keepdims=True) + NORM_EPS)
        h_ref[pl.ds(r0, NORM_ROWS), :] = (x * rstd * gain + shift).astype(BF16)
        return carry

    lax.fori_loop(0, x_ref.shape[0] // NORM_ROWS, group, 0, unroll=8)


def _ada_kernel(c_ref, w_ref, b_ref, o_ref):
    a = _silu(c_ref[...]).astype(BF16)
    o_ref[...] = _dot(a, w_ref[...].astype(BF16)) + b_ref[...]


def _ada(cvec, w_ada, b_ada):
    depth, d, n = w_ada.shape
    tn = _pick(n, (1024, 512, 256, 128))
    return pl.pallas_call(
        _ada_kernel,
        grid=(depth, n // tn),
        in_specs=[
            pl.BlockSpec((8, d), lambda l, j: (0, 0)),
            pl.BlockSpec((None, d, tn), lambda l, j: (l, 0, j)),
            pl.BlockSpec((None, 1, tn), lambda l, j: (l, 0, j)),
        ],
        out_specs=pl.BlockSpec((None, 8, tn), lambda l, j: (l, 0, j)),
        out_shape=jax.ShapeDtypeStruct((depth, 8, n), F32),
        compiler_params=_params("parallel", "parallel"),
        name="ada",
    )(cvec, w_ada, b_ada.reshape(depth, 1, n))


def _ffn_in_kernel(x_ref, mod_ref, nw_ref, wa_ref, wb_ref, g_ref, h_ref, *, mod0):
    @pl.when(pl.program_id(1) == 0)
    def _():
        _modulated_norm(h_ref, x_ref, nw_ref[...], mod_ref[mod0:mod0 + 1, :], mod_ref[mod0 + 1:mod0 + 2, :])

    h = h_ref[...]
    a = _dot(h, wa_ref[...].astype(BF16))
    b = _dot(h, wb_ref[...].astype(BF16))
    g_ref[...] = (_silu(a) * b).astype(BF16)


def _ffn_out_kernel(x_ref, g_ref, mod_ref, w_ref, o_ref, *, mod0):
    gate = 0.5 * mod_ref[mod0 + 2:mod0 + 3, :]
    o_ref[...] = x_ref[...] + gate * _dot(g_ref[...], w_ref[...].astype(BF16))


def _ffn(x, mod, nw, w_in, w_out, *, l, s, mod0, rows_per_group, tm):
    n, d = x.shape
    f = w_out.shape[2]
    fc = _pick(f, (512, 256, 128))
    nf = f // fc
    tn = _pick(d, (256, 128))
    tiles_per_group = rows_per_group // tm
    g = pl.pallas_call(
        functools.partial(_ffn_in_kernel, mod0=mod0),
        grid=(n // tm, nf),
        in_specs=[
            pl.BlockSpec((tm, d), lambda i, j: (i, 0)),
            pl.BlockSpec((None, N_MOD, d), lambda i, j: (i // tiles_per_group, 0, 0)),
            pl.BlockSpec((1, d), lambda i, j: (0, 0)),
            pl.BlockSpec((None, None, d, fc), lambda i, j: (l, s, 0, j)),
            pl.BlockSpec((None, None, d, fc), lambda i, j: (l, s, 0, nf + j)),
        ],
        out_specs=pl.BlockSpec((tm, fc), lambda i, j: (i, j)),
        out_shape=jax.ShapeDtypeStruct((n, f), BF16),
        scratch_shapes=[pltpu.VMEM((tm, d), BF16)],
        compiler_params=_params("parallel", "arbitrary"),
        name="ffn_in",
    )(x, mod, nw, w_in, w_in)
    return pl.pallas_call(
        functools.partial(_ffn_out_kernel, mod0=mod0),
        grid=(n // tm, d // tn),
        in_specs=[
            pl.BlockSpec((tm, tn), lambda i, j: (i, j)),
            pl.BlockSpec((tm, f), lambda i, j: (i, 0)),
            pl.BlockSpec((None, N_MOD, tn), lambda i, j: (i // tiles_per_group, 0, j)),
            pl.BlockSpec((None, None, f, tn), lambda i, j: (l, s, 0, j)),
        ],
        out_specs=pl.BlockSpec((tm, tn), lambda i, j: (i, j)),
        out_shape=jax.ShapeDtypeStruct((n, d), F32),
        compiler_params=_params("parallel", "parallel"),
        name="ffn_out",
    )(x, g, mod, w_out)


def _final_norm_kernel(x_ref, g_ref, o_ref):
    o_ref[...] = _rms(x_ref[...], g_ref[...])


def _final_norm(x, g, *, tm):
    n, d = x.shape
    return pl.pallas_call(
        _final_norm_kernel,
        grid=(n // tm,),
        in_specs=[pl.BlockSpec((tm, d), lambda i: (i, 0)), pl.BlockSpec((1, d), lambda i: (0, 0))],
        out_specs=pl.BlockSpec((tm, d), lambda i: (i, 0)),
        out_shape=jax.ShapeDtypeStruct((n, d), F32),
        compiler_params=_params("parallel"),
        name="final_norm",
    )(x, g)


def _inproj_kernel(x_ref, mod_ref, nw_ref, w_ref, o_ref, h_ref):
    @pl.when(pl.program_id(1) == 0)
    def _():
        _modulated_norm(h_ref, x_ref, nw_ref[...], mod_ref[3:4, :], mod_ref[4:5, :])

    o_ref[...] = _dot_nt(h_ref[...], w_ref[...]).astype(BF16)


def _inproj(x, mod, nw, w, *, l, rows_per_group, tm):
    n, d = x.shape
    nw_cols = w.shape[1]
    tn = _pick(nw_cols, (512, 256, 128))
    tiles_per_group = rows_per_group // tm
    return pl.pallas_call(
        _inproj_kernel,
        grid=(n // tm, nw_cols // tn),
        in_specs=[
            pl.BlockSpec((tm, d), lambda i, j: (i, 0)),
            pl.BlockSpec((None, N_MOD, d), lambda i, j: (i // tiles_per_group, 0, 0)),
            pl.BlockSpec((1, d), lambda i, j: (0, 0)),
            pl.BlockSpec((None, tn, d), lambda i, j: (l, j, 0)),
        ],
        out_specs=pl.BlockSpec((tm, tn), lambda i, j: (i, j)),
        out_shape=jax.ShapeDtypeStruct((n, nw_cols), BF16),
        scratch_shapes=[pltpu.VMEM((tm, d), BF16)],
        compiler_params=_params("parallel", "arbitrary"),
        name="inproj",
    )(x, mod, nw, w)


class _Cols:
    def __init__(self, d):
        self.na_q = 0
        self.na_k = NA_W
        self.na_v = 2 * NA_W
        self.cq = 3 * NA_W
        self.ckv = self.cq + MLA_Q_RANK
        self.gates = self.ckv + MLA_KV_RANK
        self.dq = self.gates + 3 * d
        self.dk = self.dq + DIFF_W
        self.dv = self.dk + DIFF_W
        self.kpe = self.dv + DIFF_W
        self.total = self.kpe + LANES


def _pack_w_in_kernel(w_ref, o_ref):
    w = w_ref[...]
    o = 3 * NA_W + MLA_Q_RANK + MLA_KV_RANK
    o2 = o + MLA_ROPE_DIM
    o3 = o2 + 3 * DIFF_W
    pad = jnp.zeros((LANES - MLA_ROPE_DIM, w.shape[1]), w.dtype)
    o_ref[...] = jnp.concatenate([w[:o], w[o3:], w[o2:o3], w[o:o2], pad], axis=0).astype(BF16)


def _pack_w_in(w_in):
    depth, d, n = w_in.shape
    tc = _pick(d, (256, 128))
    n_out = n + LANES - MLA_ROPE_DIM
    return pl.pallas_call(
        _pack_w_in_kernel,
        grid=(depth, d // tc),
        in_specs=[pl.BlockSpec((None, n, tc), lambda l, i: (l, 0, i))],
        out_specs=pl.BlockSpec((None, n_out, tc), lambda l, i: (l, 0, i)),
        out_shape=jax.ShapeDtypeStruct((depth, n_out, d), BF16),
        compiler_params=_params("parallel", "parallel"),
        name="pack_w_in",
    )(jnp.swapaxes(w_in, 1, 2))


def _softmax_pv(parts):
    mx = parts[0][0].max(axis=-1, keepdims=True)
    for s, _ in parts[1:]:
        mx = jnp.maximum(mx, s.max(axis=-1, keepdims=True))
    acc = None
    den = None
    for s, v in parts:
        e = jnp.exp(s - mx)
        den_i = e.sum(axis=-1, keepdims=True)
        o_i = _dot(e.astype(BF16), v)
        acc = o_i if acc is None else acc + o_i
        den = den_i if den is None else den + den_i
    return acc * (1.0 / den)


NA_ROWS = 4
NA_UNION = NA_ROWS + NA_KH


def _na_union_start(r, n_rows):
    if isinstance(r, int):
        return min(min(max(r - NA_KH // 2, 0), n_rows - NA_KH), n_rows - NA_UNION)
    return jnp.minimum(jnp.clip(r - NA_KH // 2, 0, n_rows - NA_KH), n_rows - NA_UNION)


def _na_kernel(q_ref, k_ref, v_ref, kc_ref, vc_ref, bias_ref, o_ref, *, n_rows):
    u0 = _na_union_start(pl.program_id(1) * NA_ROWS, n_rows)
    k0 = pl.multiple_of(u0 * GRID_W, GRID_W)
    n_loc = NA_UNION * GRID_W
    c = NA_SCALE * LOG2E
    cs = [slice(h * HEAD_DIM, (h + 1) * HEAD_DIM) for h in range(N_HEADS_NA)]
    s_loc = [_dot_nt(q_ref[:, cs[h]], k_ref[pl.ds(k0, n_loc), cs[h]]) * c + bias_ref[h] for h in range(N_HEADS_NA)]
    s_ctx = [_dot_nt(q_ref[:, cs[h]], kc_ref[:, cs[h]]) * c for h in range(N_HEADS_NA)]
    for h in range(N_HEADS_NA):
        m = jnp.maximum(s_loc[h].max(axis=-1, keepdims=True), s_ctx[h].max(axis=-1, keepdims=True))
        e_loc = jnp.exp2(s_loc[h] - m)
        e_ctx = jnp.exp2(s_ctx[h] - m)
        den = e_loc.sum(axis=-1, keepdims=True) + e_ctx.sum(axis=-1, keepdims=True)
        o = _dot(e_loc.astype(BF16), v_ref[pl.ds(k0, n_loc), cs[h]]) + _dot(e_ctx.astype(BF16), vc_ref[:, cs[h]])
        o_ref[:, cs[h]] = (o * (1.0 / den)).astype(BF16)


def _na_bias(rpb, n_rows):
    assert n_rows >= NA_UNION and n_rows % NA_ROWS == 0
    h, n_ro, _ = rpb.shape
    w = GRID_W
    col = jnp.arange(w)
    c0 = jnp.clip(col - NA_KW // 2, 0, w - NA_KW)
    col_ok = (col[None, :] >= c0[:, None]) & (col[None, :] < c0[:, None] + NA_KW)
    edge = w - NA_KW
    rpb = rpb.astype(F32)
    text = jnp.concatenate([jnp.broadcast_to(rpb[..., :1], (h, n_ro, edge)), rpb,
                            jnp.broadcast_to(rpb[..., -1:], (h, n_ro, edge)),
                            jnp.zeros((h, n_ro, 1), F32)], axis=-1)
    skew = jnp.tile(text, (1, 1, w))[..., :w * (2 * w - 1)].reshape(h, n_ro, w, 2 * w - 1)
    t = jnp.where(col_ok[None, None], skew[..., w - 1:], NEG_INF) * LOG2E
    masked = jnp.full((h, w, w), NEG_INF * LOG2E, F32)
    kinds = []
    for r in (0, NA_ROWS, n_rows - NA_ROWS):
        u0 = _na_union_start(r, n_rows)
        q_rows = []
        for qr in range(r, r + NA_ROWS):
            r0 = min(max(qr - NA_KH // 2, 0), n_rows - NA_KH)
            tiles = [t[:, kr - qr + NA_KH - 1] if r0 <= kr < r0 + NA_KH else masked
                     for kr in range(u0, u0 + NA_UNION)]
            q_rows.append(jnp.concatenate(tiles, axis=-1))
        kinds.append(jnp.concatenate(q_rows, axis=1))
    return jnp.stack(kinds, axis=0)


def _na_attn(px, pc, bias, cols, *, batch, seq, ctx):
    n_rows = seq // GRID_W
    steps = n_rows // NA_ROWS
    tq = NA_ROWS * GRID_W
    qb, kb, vb = cols.na_q // NA_W, cols.na_k // NA_W, cols.na_v // NA_W
    kind = lambda i: jnp.where(i == 0, 0, jnp.where(i == steps - 1, 2, 1))
    return pl.pallas_call(
        functools.partial(_na_kernel, n_rows=n_rows),
        grid=(batch, steps),
        in_specs=[
            pl.BlockSpec((tq, NA_W), lambda b, i: (b * steps + i, qb)),
            pl.BlockSpec((seq, NA_W), lambda b, i: (b, kb)),
            pl.BlockSpec((seq, NA_W), lambda b, i: (b, vb)),
            pl.BlockSpec((ctx, NA_W), lambda b, i: (b, kb)),
            pl.BlockSpec((ctx, NA_W), lambda b, i: (b, vb)),
            pl.BlockSpec((None,) + bias.shape[1:], lambda b, i: (kind(i), 0, 0, 0)),
        ],
        out_specs=pl.BlockSpec((tq, NA_W), lambda b, i: (b * steps + i, 0)),
        out_shape=jax.ShapeDtypeStruct((batch * seq, NA_W), BF16),
        compiler_params=_params("parallel", "arbitrary"),
        name="na_attn",
    )(px, px, px, pc, pc, bias)


def _ctx_na_kernel(q_ref, k_ref, v_ref, o_ref):
    for h in range(N_HEADS_NA):
        cs = slice(h * HEAD_DIM, (h + 1) * HEAD_DIM)
        s = _dot_nt(q_ref[:, cs], k_ref[:, cs]) * NA_SCALE
        o_ref[:, cs] = _softmax_pv([(s, v_ref[:, cs])]).astype(BF16)


def _ctx_na_attn(pc, cols, *, batch, ctx):
    qb, kb, vb = cols.na_q // NA_W, cols.na_k // NA_W, cols.na_v // NA_W
    return pl.pallas_call(
        _ctx_na_kernel,
        grid=(batch,),
        in_specs=[
            pl.BlockSpec((ctx, NA_W), lambda b: (b, qb)),
            pl.BlockSpec((ctx, NA_W), lambda b: (b, kb)),
            pl.BlockSpec((ctx, NA_W), lambda b: (b, vb)),
        ],
        out_specs=pl.BlockSpec((ctx, NA_W), lambda b: (b, 0)),
        out_shape=jax.ShapeDtypeStruct((batch * ctx, NA_W), BF16),
        compiler_params=_params("parallel"),
        name="ctx_na_attn",
    )(pc, pc, pc)


VT_ROWS = 144


def _attn_pipeline_step(qs, k_scr, vt_scr, park, parked):
    n_sets = park[0].shape[0] if park is not None else parked[0].shape[0]
    n_keys = k_scr.shape[0]
    ch = _pick(n_keys, (2 * LANES, LANES))
    run = [None] * n_sets
    acc = [None] * n_sets
    for c in range(n_keys // ch):
        ks = slice(c * ch, (c + 1) * ch)
        for t in range(n_sets):
            if park is not None:
                s = _dot_nt(k_scr[ks, :], qs[t])
                park[0][t, ks, :] = s
                m = s.reshape(ch // 8, 8, s.shape[1]).max(axis=0)
                run[t] = m if run[t] is None else jnp.maximum(run[t], m)
            if parked is not None:
                sp = parked[0][t, ks, :]
                e = jnp.exp2(sp.reshape(ch // 8, 8, sp.shape[1]) - parked[1][t][None]).reshape(sp.shape)
                pv = _dot(vt_scr[:, ks], e.astype(BF16))
                acc[t] = pv if acc[t] is None else acc[t] + pv
    if park is not None:
        for t in range(n_sets):
            park[1][t] = jnp.broadcast_to(run[t].max(axis=0, keepdims=True), run[t].shape)
    return acc


def _attn_fill_vt(vt_scr, col0, v):
    n = v.shape[0]
    vt_scr[0:HEAD_DIM, col0:col0 + n] = v.T.astype(BF16)
    vt_scr[HEAD_DIM:, col0:col0 + n] = jnp.ones((VT_ROWS - HEAD_DIM, n), BF16)


def _attn_pipeline(step, last, bufs, body):
    @pl.when(step == 0)
    def _():
        body(bufs[0], None)

    if last > 1:
        @pl.when((step > 0) & (step < last) & (step % 2 == 0))
        def _():
            body(bufs[0], bufs[1])

        @pl.when((step < last) & (step % 2 == 1))
        def _():
            body(bufs[1], bufs[0])

    @pl.when(step == last)
    def _():
        body(None, bufs[(last - 1) % 2])


def _mla_q_kernel(*refs, latent):
    if latent:
        cq_ref, cos_ref, sin_ref, qn_ref, wuq_ref, o_ref, n_scr = refs
    else:
        cq_ref, qn_ref, wuq_ref, o_ref, n_scr = refs

    @pl.when(pl.program_id(1) == 0)
    def _():
        n_scr[...] = _rms(cq_ref[...].astype(F32), qn_ref[...]).astype(BF16)

    q = _dot(n_scr[...], wuq_ref[...])
    q_pe = q[:, MLA_NOPE_DIM:]
    if latent:
        q_pe = _rope(q_pe, cos_ref[...], sin_ref[...])
    o_ref[...] = (jnp.concatenate([q[:, :MLA_NOPE_DIM], q_pe], axis=1) * (MLA_SCALE * LOG2E)).astype(BF16)


def _mla_q(pq, cos, sin, qn, wuq, cols, *, n_q, latent):
    n = pq.shape[0]
    tm = _pick(n_q, (512, 256, 128))
    tiles = n_q // tm
    cqb = cols.cq // MLA_Q_RANK
    in_specs = [pl.BlockSpec((tm, MLA_Q_RANK), lambda i, h: (i, cqb))]
    args = [pq]
    if latent:
        in_specs += [pl.BlockSpec((tm, LANES), lambda i, h: (i % tiles, 0))] * 2
        args += [cos, sin]
    in_specs += [pl.BlockSpec((1, MLA_Q_RANK), lambda i, h: (0, 0)),
                 pl.BlockSpec((None, MLA_Q_RANK, 2 * LANES), lambda i, h: (h, 0, 0))]
    args += [qn, wuq]
    return pl.pallas_call(
        functools.partial(_mla_q_kernel, latent=latent),
        grid=(n // tm, N_HEADS_MLA),
        in_specs=in_specs,
        out_specs=pl.BlockSpec((tm, 2 * LANES), lambda i, h: (i, h)),
        out_shape=jax.ShapeDtypeStruct((n, N_HEADS_MLA * 2 * LANES), BF16),
        scratch_shapes=[pltpu.VMEM((tm, MLA_Q_RANK), BF16)],
        compiler_params=_params("parallel", "arbitrary"),
        name="mla_q" if latent else "ctx_mla_q",
    )(*args)


def _mla_kernel(*refs, ctx, seq, latent, last):
    if latent:
        (q_ref, ckvc_ref, kpec_ref, ckvl_ref, kpel_ref, cosk_ref, sink_ref,
         kvn_ref, wukv_ref, o_ref, k_scr, v_scr, s0, m0, s1, m1) = refs
    else:
        q_ref, ckvc_ref, kpec_ref, kvn_ref, wukv_ref, o_ref, k_scr, v_scr, s0, m0, s1, m1 = refs
    step = pl.program_id(2)

    @pl.when(step == 0)
    def _():
        w = wukv_ref[...]

        def fill(row0, n, ckv_ref, kpe):
            kv = _dot(_rms(ckv_ref[...].astype(F32), kvn_ref[...]).astype(BF16), w)
            k_scr[row0:row0 + n, 0:MLA_NOPE_DIM] = kv[:, :MLA_NOPE_DIM].astype(BF16)
            k_scr[row0:row0 + n, MLA_NOPE_DIM:] = kpe.astype(BF16)
            _attn_fill_vt(v_scr, row0, kv[:, MLA_NOPE_DIM:])

        fill(0, ctx, ckvc_ref, kpec_ref[...])
        if latent:
            fill(ctx, seq, ckvl_ref, _rope(kpel_ref[...].astype(F32), cosk_ref[...], sink_ref[...]))

    def body(park, parked):
        qs = [q_ref[...]] if park is not None else None
        acc = _attn_pipeline_step(qs, k_scr, v_scr, park, parked)
        if parked is not None:
            o_t = acc[0][:MLA_V_DIM] * (1.0 / acc[0][MLA_V_DIM:MLA_V_DIM + 1])
            o_ref[...] = o_t.T.astype(BF16)

    _attn_pipeline(step, last, ((s0, m0), (s1, m1)), body)


def _mla_attn(q, pc, px, cos, sin, kvn, wukv, cols, *, batch, n_q, ctx, seq, latent):
    tq = _pick(n_q, (256, 128))
    nq = n_q // tq
    ckvb, kpeb = cols.ckv // MLA_KV_RANK, cols.kpe // LANES
    n_keys = ctx + (seq if latent else 0)
    q_tile = lambda i: jnp.minimum(i, nq - 1)
    o_tile = lambda i: jnp.maximum(i - 1, 0)
    in_specs = [
        pl.BlockSpec((tq, 2 * LANES), lambda b, h, i: (b * nq + q_tile(i), h)),
        pl.BlockSpec((ctx, MLA_KV_RANK), lambda b, h, i: (b, ckvb)),
        pl.BlockSpec((ctx, LANES), lambda b, h, i: (b, kpeb)),
    ]
    args = [q, pc, pc]
    if latent:
        in_specs += [
            pl.BlockSpec((seq, MLA_KV_RANK), lambda b, h, i: (b, ckvb)),
            pl.BlockSpec((seq, LANES), lambda b, h, i: (b, kpeb)),
            pl.BlockSpec((seq, LANES), lambda b, h, i: (0, 0)),
            pl.BlockSpec((seq, LANES), lambda b, h, i: (0, 0)),
        ]
        args += [px, px, cos, sin]
    in_specs += [
        pl.BlockSpec((1, MLA_KV_RANK), lambda b, h, i: (0, 0)),
        pl.BlockSpec((None, MLA_KV_RANK, 2 * LANES), lambda b, h, i: (h, 0, 0)),
    ]
    args += [kvn, wukv]
    return pl.pallas_call(
        functools.partial(_mla_kernel, ctx=ctx, seq=seq, latent=latent, last=nq),
        grid=(batch, N_HEADS_MLA, nq + 1),
        in_specs=in_specs,
        out_specs=pl.BlockSpec((tq, MLA_V_DIM), lambda b, h, i: (b * nq + o_tile(i), h)),
        out_shape=jax.ShapeDtypeStruct((batch * n_q, MLA_W), BF16),
        scratch_shapes=[pltpu.VMEM((n_keys, 2 * LANES), BF16), pltpu.VMEM((VT_ROWS, n_keys), BF16)]
        + 2 * [pltpu.VMEM((1, n_keys, tq), F32), pltpu.VMEM((1, 8, tq), F32)],
        compiler_params=_params("parallel", "parallel", "arbitrary"),
        name="mla_attn" if latent else "ctx_mla_attn",
    )(*args)


def _diff_kernel(*refs, ctx, seq, latent, lambda_init, last):
    if latent:
        (q_ref, kc_ref, vc_ref, kl_ref, vl_ref, cosq_ref, sinq_ref, cosk_ref, sink_ref,
         lam_ref, sub_ref, o_ref, k_scr, v_scr, s0, m0, s1, m1) = refs
    else:
        q_ref, kc_ref, vc_ref, lam_ref, sub_ref, o_ref, k_scr, v_scr, s0, m0, s1, m1 = refs
    step = pl.program_id(2)

    @pl.when(step == 0)
    def _():
        k_scr[0:ctx, :] = kc_ref[...]
        _attn_fill_vt(v_scr, 0, vc_ref[...].astype(F32))
        if latent:
            k_scr[ctx:ctx + seq, :] = _rope(kl_ref[...].astype(F32), cosk_ref[...], sink_ref[...]).astype(BF16)
            _attn_fill_vt(v_scr, ctx, vl_ref[...].astype(F32))

    def body(park, parked):
        qs = None
        if park is not None:
            q = q_ref[...].astype(F32)
            if latent:
                q = _rope(q, cosq_ref[...], sinq_ref[...])
            q = q * (DIFF_SCALE * LOG2E)
            lane = lax.broadcasted_iota(jnp.int32, q.shape, 1)
            qs = [jnp.where(lane < DIFF_QK_DIM, q, 0.0).astype(BF16),
                  jnp.where(lane >= DIFF_QK_DIM, q, 0.0).astype(BF16)]
        acc = _attn_pipeline_step(qs, k_scr, v_scr, park, parked)
        if parked is not None:
            a1, a2 = acc
            dl = lam_ref[...]
            lam = (jnp.exp(jnp.sum(dl[0:1] * dl[1:2], axis=-1, keepdims=True))
                   - jnp.exp(jnp.sum(dl[2:3] * dl[3:4], axis=-1, keepdims=True)) + lambda_init)
            o_t = (a1[:HEAD_DIM] * (1.0 / a1[HEAD_DIM:HEAD_DIM + 1])
                   - a2[:HEAD_DIM] * (lam / a2[HEAD_DIM:HEAD_DIM + 1]))
            o_ref[...] = (_rms(o_t.T, sub_ref[...]) * (1.0 - lambda_init)).astype(BF16)

    _attn_pipeline(step, last, ((s0, m0), (s1, m1)), body)


def _diff_attn(pq, pc, px, cos, sin, dlam, subln, cols, *, batch, n_q, ctx, seq, latent, lambda_init):
    tq = _pick(n_q, (256, 128))
    nq = n_q // tq
    qb, kb, vb = cols.dq // HEAD_DIM, cols.dk // HEAD_DIM, cols.dv // HEAD_DIM
    n_keys = ctx + (seq if latent else 0)
    q_tile = lambda i: jnp.minimum(i, nq - 1)
    o_tile = lambda i: jnp.maximum(i - 1, 0)
    in_specs = [
        pl.BlockSpec((tq, HEAD_DIM), lambda b, h, i: (b * nq + q_tile(i), qb + h)),
        pl.BlockSpec((ctx, HEAD_DIM), lambda b, h, i: (b, kb + h)),
        pl.BlockSpec((ctx, HEAD_DIM), lambda b, h, i: (b, vb + h)),
    ]
    args = [pq, pc, pc]
    if latent:
        in_specs += [
            pl.BlockSpec((seq, HEAD_DIM), lambda b, h, i: (b, kb + h)),
            pl.BlockSpec((seq, HEAD_DIM), lambda b, h, i: (b, vb + h)),
            pl.BlockSpec((tq, LANES), lambda b, h, i: (q_tile(i), 0)),
            pl.BlockSpec((tq, LANES), lambda b, h, i: (q_tile(i), 0)),
            pl.BlockSpec((seq, LANES), lambda b, h, i: (0, 0)),
            pl.BlockSpec((seq, LANES), lambda b, h, i: (0, 0)),
        ]
        args += [px, px, cos, sin, cos, sin]
    in_specs += [
        pl.BlockSpec((4, DIFF_QK_DIM), lambda b, h, i: (0, 0)),
        pl.BlockSpec((1, HEAD_DIM), lambda b, h, i: (0, 0)),
    ]
    args += [dlam, subln]
    return pl.pallas_call(
        functools.partial(_diff_kernel, ctx=ctx, seq=seq, latent=latent, lambda_init=lambda_init, last=nq),
        grid=(batch, N_HEADS_DIFF, nq + 1),
        in_specs=in_specs,
        out_specs=pl.BlockSpec((tq, HEAD_DIM), lambda b, h, i: (b * nq + o_tile(i), h)),
        out_shape=jax.ShapeDtypeStruct((batch * n_q, DIFF_W), BF16),
        scratch_shapes=[pltpu.VMEM((n_keys, HEAD_DIM), BF16), pltpu.VMEM((VT_ROWS, n_keys), BF16)]
        + 2 * [pltpu.VMEM((2, n_keys, tq), F32), pltpu.VMEM((2, 8, tq), F32)],
        compiler_params=_params("parallel", "parallel", "arbitrary"),
        name="diff_attn" if latent else "ctx_diff_attn",
    )(*args)


def _merge_kernel(oa_ref, ob_ref, od_ref, ga_ref, gb_ref, gd_ref, wa_ref, wb_ref, wd_ref, y_ref):
    y = (_sigmoid(ga_ref[...].astype(F32)) * _dot(oa_ref[...], wa_ref[...])
         + _sigmoid(gb_ref[...].astype(F32)) * _dot(ob_ref[...], wb_ref[...])
         + _sigmoid(gd_ref[...].astype(F32)) * _dot(od_ref[...], wd_ref[...]))
    y_ref[...] = y.astype(BF16)


def _merge(o_na, o_mla, o_diff, p, wb_na, wb_mla, wb_diff, cols, *, tm):
    n = o_na.shape[0]
    d = wb_na.shape[1]
    tn = _pick(d, (512, 256, 128))
    g0 = cols.gates // tn
    gs = d // tn
    return pl.pallas_call(
        _merge_kernel,
        grid=(n // tm, d // tn),
        in_specs=[
            pl.BlockSpec((tm, NA_W), lambda i, j: (i, 0)),
            pl.BlockSpec((tm, MLA_W), lambda i, j: (i, 0)),
            pl.BlockSpec((tm, DIFF_W), lambda i, j: (i, 0)),
            pl.BlockSpec((tm, tn), lambda i, j: (i, g0 + j)),
            pl.BlockSpec((tm, tn), lambda i, j: (i, g0 + gs + j)),
            pl.BlockSpec((tm, tn), lambda i, j: (i, g0 + 2 * gs + j)),
            pl.BlockSpec((NA_W, tn), lambda i, j: (0, j)),
            pl.BlockSpec((MLA_W, tn), lambda i, j: (0, j)),
            pl.BlockSpec((DIFF_W, tn), lambda i, j: (0, j)),
        ],
        out_specs=pl.BlockSpec((tm, tn), lambda i, j: (i, j)),
        out_shape=jax.ShapeDtypeStruct((n, d), BF16),
        compiler_params=_params("parallel", "parallel"),
        name="merge",
    )(o_na, o_mla, o_diff, p, p, p, wb_na, wb_mla, wb_diff)


def _outproj_kernel(x_ref, y_ref, mod_ref, w_ref, o_ref):
    o_ref[...] = x_ref[...] + mod_ref[...] * _dot(y_ref[...], w_ref[...].astype(BF16))


def _outproj(x, y, gate, w_out, *, l, rows_per_group, tm):
    n, d = x.shape
    tn = _pick(d, (512, 256, 128))
    tiles_per_group = rows_per_group // tm
    return pl.pallas_call(
        _outproj_kernel,
        grid=(n // tm, d // tn),
        in_specs=[
            pl.BlockSpec((tm, tn), lambda i, j: (i, j)),
            pl.BlockSpec((tm, d), lambda i, j: (i, 0)),
            pl.BlockSpec((None, 1, tn), lambda i, j: (i // tiles_per_group, 0, j)),
            pl.BlockSpec((None, d, tn), lambda i, j: (l, 0, j)),
        ],
        out_specs=pl.BlockSpec((tm, tn), lambda i, j: (i, j)),
        out_shape=jax.ShapeDtypeStruct((n, d), F32),
        compiler_params=_params("parallel", "parallel"),
        name="outproj",
    )(x, y, gate, w_out)


def _rope_tables(seq):
    quarter = MLA_ROPE_DIM // 4
    freqs = ROPE_THETA ** (-jnp.arange(quarter, dtype=F32) / quarter)
    t = jnp.arange(seq)
    rows, cols = t // GRID_W, t % GRID_W

    def unit(pos):
        ang = pos.astype(F32)[:, None] * freqs
        c, s = jnp.cos(ang), jnp.sin(ang)
        return jnp.concatenate([c, c], axis=1), jnp.concatenate([-s, s], axis=1)

    cr, sr = unit(rows)
    cc, sc = unit(cols)
    cos = jnp.concatenate([cr, cc], axis=1)
    sin = jnp.concatenate([sr, sc], axis=1)
    reps = LANES // MLA_ROPE_DIM
    return jnp.tile(cos, (1, reps)), jnp.tile(sin, (1, reps))


def _pack_heads(w, n_heads, width):
    k = w.shape[0]
    per = w.shape[1] // n_heads
    w = jnp.moveaxis(w.reshape(k, n_heads, per), 1, 0)
    if per < width:
        w = jnp.concatenate([w, jnp.zeros((n_heads, k, width - per), w.dtype)], axis=2)
    return w.astype(BF16)


def kernel(x, c, ctx, c_ctx, w_ada, b_ada, norm_w, ffn_w_in, ffn_w_out, w_in, na_rpb, mla_q_norm, mla_kv_norm,
           mla_w_uq, mla_w_ukv, diff_lambda, diff_subln, w_branch, w_out, final_norm):
    batch, seq, d = x.shape
    n_ctx = ctx.shape[1]
    depth = w_ada.shape[0]
    cols = _Cols(d)
    tm_x = _pick(seq, (1024, 512, 256, 128))
    tm_c = batch * n_ctx

    xs = x.reshape(batch * seq, d)
    xc = ctx.reshape(batch * n_ctx, d)
    cvec = jnp.concatenate([c, c_ctx[None, :], jnp.zeros((8 - batch - 1, d), F32)], axis=0)
    mods = _ada(cvec, w_ada, b_ada).reshape(depth, 8, N_MOD, d)
    cos, sin = _rope_tables(seq)
    wp = _pack_w_in(w_in)

    for l in range(depth):
        last = l == depth - 1
        lambda_init = 0.8 - 0.6 * math.exp(-0.3 * l)
        m = mods[l, :batch]
        mc = mods[l, batch:batch + 1]
        nw = norm_w[l].reshape(3, 1, d)

        xs = _ffn(xs, m, nw[0], ffn_w_in, ffn_w_out, l=l, s=0, mod0=0, rows_per_group=seq, tm=tm_x)
        xc = _ffn(xc, mc, nw[0], ffn_w_in, ffn_w_out, l=l, s=0, mod0=0, rows_per_group=tm_c, tm=tm_c)

        px = _inproj(xs, m, nw[1], wp, l=l, rows_per_group=seq, tm=tm_x)
        pc = _inproj(xc, mc, nw[1], wp, l=l, rows_per_group=tm_c, tm=tm_c)

        bias = _na_bias(na_rpb[l], seq // GRID_W)
        qn = mla_q_norm[l].reshape(1, MLA_Q_RANK)
        kvn = mla_kv_norm[l].reshape(1, MLA_KV_RANK)
        wuq = _pack_heads(mla_w_uq[l], N_HEADS_MLA, 2 * LANES)
        wukv = _pack_heads(mla_w_ukv[l], N_HEADS_MLA, 2 * LANES)
        subln = diff_subln[l].reshape(1, HEAD_DIM)
        wb = w_branch[l].astype(BF16)
        wb_na, wb_mla, wb_diff = wb[:NA_W], wb[NA_W:NA_W + MLA_W], wb[NA_W + MLA_W:]

        o_na = _na_attn(px, pc, bias, cols, batch=batch, seq=seq, ctx=n_ctx)
        q_mla = _mla_q(px, cos, sin, qn, wuq, cols, n_q=seq, latent=True)
        o_mla = _mla_attn(q_mla, pc, px, cos, sin, kvn, wukv, cols,
                          batch=batch, n_q=seq, ctx=n_ctx, seq=seq, latent=True)
        o_diff = _diff_attn(px, pc, px, cos, sin, diff_lambda[l], subln, cols,
                            batch=batch, n_q=seq, ctx=n_ctx, seq=seq, latent=True, lambda_init=lambda_init)
        y = _merge(o_na, o_mla, o_diff, px, wb_na, wb_mla, wb_diff, cols, tm=tm_x)
        xs = _outproj(xs, y, m[:, 5:6], w_out, l=l, rows_per_group=seq, tm=tm_x)
        xs = _ffn(xs, m, nw[2], ffn_w_in, ffn_w_out, l=l, s=1, mod0=6, rows_per_group=seq, tm=tm_x)

        if not last:
            oc_na = _ctx_na_attn(pc, cols, batch=batch, ctx=n_ctx)
            qc_mla = _mla_q(pc, cos, sin, qn, wuq, cols, n_q=n_ctx, latent=False)
            oc_mla = _mla_attn(qc_mla, pc, px, cos, sin, kvn, wukv, cols,
                               batch=batch, n_q=n_ctx, ctx=n_ctx, seq=seq, latent=False)
            oc_diff = _diff_attn(pc, pc, px, cos, sin, diff_lambda[l], subln, cols,
                                 batch=batch, n_q=n_ctx, ctx=n_ctx, seq=seq, latent=False, lambda_init=lambda_init)
            yc = _merge(oc_na, oc_mla, oc_diff, pc, wb_na, wb_mla, wb_diff, cols, tm=tm_c)
            xc = _outproj(xc, yc, mc[:, 5:6], w_out, l=l, rows_per_group=tm_c, tm=tm_c)
            xc = _ffn(xc, mc, nw[2], ffn_w_in, ffn_w_out, l=l, s=1, mod0=6, rows_per_group=tm_c, tm=tm_c)

    return _final_norm(xs, final_norm.reshape(1, d), tm=tm_x).reshape(batch, seq, d)
```

```python
import functools
import math

import jax
import jax.numpy as jnp
from jax import lax
from jax.experimental import pallas as pl
from jax.experimental.pallas import tpu as pltpu

GRID_W = 64
HEAD_DIM = 128
N_HEADS_NA = 6
N_HEADS_MLA = 5
N_HEADS_DIFF = 5
NA_KH = 8
NA_KW = 16
MLA_Q_RANK = 768
MLA_KV_RANK = 512
MLA_NOPE_DIM = 128
MLA_ROPE_DIM = 64
MLA_V_DIM = 128
DIFF_QK_DIM = HEAD_DIM // 2
N_MOD = 9
ROPE_THETA = 10000.0
NORM_EPS = 1e-6
NEG_INF = -1e30
NA_W = N_HEADS_NA * HEAD_DIM
MLA_W = N_HEADS_MLA * MLA_V_DIM
DIFF_W = N_HEADS_DIFF * HEAD_DIM
NA_SCALE = HEAD_DIM ** -0.5
MLA_SCALE = (MLA_NOPE_DIM + MLA_ROPE_DIM) ** -0.5
DIFF_SCALE = DIFF_QK_DIM ** -0.5

LOG2E = 1.4426950408889634
LANES = 128
VMEM_LIMIT = 60 * 1024 * 1024
BF16 = jnp.bfloat16
F32 = jnp.float32


def _params(*sem):
    return pltpu.CompilerParams(dimension_semantics=sem, vmem_limit_bytes=VMEM_LIMIT)


def _pick(n, prefs):
    for p in prefs:
        if n % p == 0:
            return p
    return n


def _dot(a, b):
    return jnp.dot(a, b, preferred_element_type=F32)


def _dot_nt(a, b):
    return lax.dot_general(a, b, (((1,), (1,)), ((), ())), preferred_element_type=F32)


def _rms(x, g):
    return x * lax.rsqrt(jnp.mean(x * x, axis=-1, keepdims=True) + NORM_EPS) * g


def _silu(x):
    return x * (1.0 / (1.0 + jnp.exp(-x)))


def _sigmoid(x):
    return 1.0 / (1.0 + jnp.exp(-x))


def _swap16(x):
    lane = lax.broadcasted_iota(jnp.int32, x.shape, x.ndim - 1)
    down = pltpu.roll(x, 16, x.ndim - 1)
    up = pltpu.roll(x, LANES - 16, x.ndim - 1)
    return jnp.where((lane & 16) != 0, down, up)


def _rope(x, cos, sin):
    return x * cos + _swap16(x) * sin


NORM_ROWS = 16


def _modulated_norm(h_ref, x_ref, nw, shift, scale):
    gain = nw * (1.0 + scale)

    def group(i, carry):
        r0 = pl.multiple_of(i * NORM_ROWS, NORM_ROWS)
        x = x_ref[pl.ds(r0, NORM_ROWS), :]
        rstd = lax.rsqrt(jnp.mean(x * x, axis=-1, keepdims=True) + NORM_EPS)
        h_ref[pl.ds(r0, NORM_ROWS), :] = (x * rstd * gain + shift).astype(BF16)
        return carry

    lax.fori_loop(0, x_ref.shape[0] // NORM_ROWS, group, 0, unroll=8)


def _ada_kernel(c_ref, w_ref, b_ref, o_ref):
    a = _silu(c_ref[...]).astype(BF16)
    o_ref[...] = _dot(a, w_ref[...].astype(BF16)) + b_ref[...]


def _ada(cvec, w_ada, b_ada):
    depth, d, n = w_ada.shape
    tn = _pick(n, (1024, 512, 256, 128))
    return pl.pallas_call(
        _ada_kernel,
        grid=(depth, n // tn),
        in_specs=[
            pl.BlockSpec((8, d), lambda l, j: (0, 0)),
            pl.BlockSpec((None, d, tn), lambda l, j: (l, 0, j)),
            pl.BlockSpec((None, 1, tn), lambda l, j: (l, 0, j)),
        ],
        out_specs=pl.BlockSpec((None, 8, tn), lambda l, j: (l, 0, j)),
        out_shape=jax.ShapeDtypeStruct((depth, 8, n), F32),
        compiler_params=_params("parallel", "parallel"),
        name="ada",
    )(cvec, w_ada, b_ada.reshape(depth, 1, n))


def _ffn_in_kernel(x_ref, mod_ref, nw_ref, wa_ref, wb_ref, g_ref, h_ref, *, mod0):
    @pl.when(pl.program_id(1) == 0)
    def _():
        _modulated_norm(h_ref, x_ref, nw_ref[...], mod_ref[mod0:mod0 + 1, :], mod_ref[mod0 + 1:mod0 + 2, :])

    h = h_ref[...]
    a = _dot(h, wa_ref[...].astype(BF16))
    b = _dot(h, wb_ref[...].astype(BF16))
    g_ref[...] = (_silu(a) * b).astype(BF16)


def _ffn_out_kernel(x_ref, g_ref, mod_ref, w_ref, o_ref, *, mod0):
    gate = 0.5 * mod_ref[mod0 + 2:mod0 + 3, :]
    o_ref[...] = x_ref[...] + gate * _dot(g_ref[...], w_ref[...].astype(BF16))


def _ffn(x, mod, nw, w_in, w_out, *, l, s, mod0, rows_per_group, tm):
    n, d = x.shape
    f = w_out.shape[2]
    fc = _pick(f, (512, 256, 128))
    nf = f // fc
    tn = _pick(d, (256, 128))
    tiles_per_group = rows_per_group // tm
    g = pl.pallas_call(
        functools.partial(_ffn_in_kernel, mod0=mod0),
        grid=(n // tm, nf),
        in_specs=[
            pl.BlockSpec((tm, d), lambda i, j: (i, 0)),
            pl.BlockSpec((None, N_MOD, d), lambda i, j: (i // tiles_per_group, 0, 0)),
            pl.BlockSpec((1, d), lambda i, j: (0, 0)),
            pl.BlockSpec((None, None, d, fc), lambda i, j: (l, s, 0, j)),
            pl.BlockSpec((None, None, d, fc), lambda i, j: (l, s, 0, nf + j)),
        ],
        out_specs=pl.BlockSpec((tm, fc), lambda i, j: (i, j)),
        out_shape=jax.ShapeDtypeStruct((n, f), BF16),
        scratch_shapes=[pltpu.VMEM((tm, d), BF16)],
        compiler_params=_params("parallel", "arbitrary"),
        name="ffn_in",
    )(x, mod, nw, w_in, w_in)
    return pl.pallas_call(
        functools.partial(_ffn_out_kernel, mod0=mod0),
        grid=(n // tm, d // tn),
        in_specs=[
            pl.BlockSpec((tm, tn), lambda i, j: (i, j)),
            pl.BlockSpec((tm, f), lambda i, j: (i, 0)),
            pl.BlockSpec((None, N_MOD, tn), lambda i, j: (i // tiles_per_group, 0, j)),
            pl.BlockSpec((None, None, f, tn), lambda i, j: (l, s, 0, j)),
        ],
        out_specs=pl.BlockSpec((tm, tn), lambda i, j: (i, j)),
        out_shape=jax.ShapeDtypeStruct((n, d), F32),
        compiler_params=_params("parallel", "parallel"),
        name="ffn_out",
    )(x, g, mod, w_out)


def _final_norm_kernel(x_ref, g_ref, o_ref):
    o_ref[...] = _rms(x_ref[...], g_ref[...])


def _final_norm(x, g, *, tm):
    n, d = x.shape
    return pl.pallas_call(
        _final_norm_kernel,
        grid=(n // tm,),
        in_specs=[pl.BlockSpec((tm, d), lambda i: (i, 0)), pl.BlockSpec((1, d), lambda i: (0, 0))],
        out_specs=pl.BlockSpec((tm, d), lambda i: (i, 0)),
        out_shape=jax.ShapeDtypeStruct((n, d), F32),
        compiler_params=_params("parallel"),
        name="final_norm",
    )(x, g)


def _inproj_kernel(x_ref, mod_ref, nw_ref, w_ref, o_ref, h_ref):
    @pl.when(pl.program_id(1) == 0)
    def _():
        _modulated_norm(h_ref, x_ref, nw_ref[...], mod_ref[3:4, :], mod_ref[4:5, :])

    o_ref[...] = _dot_nt(h_ref[...], w_ref[...]).astype(BF16)


def _inproj(x, mod, nw, w, *, l, rows_per_group, tm):
    n, d = x.shape
    nw_cols = w.shape[1]
    tn = _pick(nw_cols, (512, 256, 128))
    tiles_per_group = rows_per_group // tm
    return pl.pallas_call(
        _inproj_kernel,
        grid=(n // tm, nw_cols // tn),
        in_specs=[
            pl.BlockSpec((tm, d), lambda i, j: (i, 0)),
            pl.BlockSpec((None, N_MOD, d), lambda i, j: (i // tiles_per_group, 0, 0)),
            pl.BlockSpec((1, d), lambda i, j: (0, 0)),
            pl.BlockSpec((None, tn, d), lambda i, j: (l, j, 0)),
        ],
        out_specs=pl.BlockSpec((tm, tn), lambda i, j: (i, j)),
        out_shape=jax.ShapeDtypeStruct((n, nw_cols), BF16),
        scratch_shapes=[pltpu.VMEM((tm, d), BF16)],
        compiler_params=_params("parallel", "arbitrary"),
        name="inproj",
    )(x, mod, nw, w)


class _Cols:
    def __init__(self, d):
        self.na_q = 0
        self.na_k = NA_W
        self.na_v = 2 * NA_W
        self.cq = 3 * NA_W
        self.ckv = self.cq + MLA_Q_RANK
        self.gates = self.ckv + MLA_KV_RANK
        self.dq = self.gates + 3 * d
        self.dk = self.dq + DIFF_W
        self.dv = self.dk + DIFF_W
        self.kpe = self.dv + DIFF_W
        self.total = self.kpe + LANES


def _pack_w_in_kernel(w_ref, o_ref):
    w = w_ref[...]
    o = 3 * NA_W + MLA_Q_RANK + MLA_KV_RANK
    o2 = o + MLA_ROPE_DIM
    o3 = o2 + 3 * DIFF_W
    pad = jnp.zeros((LANES - MLA_ROPE_DIM, w.shape[1]), w.dtype)
    o_ref[...] = jnp.concatenate([w[:o], w[o3:], w[o2:o3], w[o:o2], pad], axis=0).astype(BF16)


def _pack_w_in(w_in):
    depth, d, n = w_in.shape
    tc = _pick(d, (256, 128))
    n_out = n + LANES - MLA_ROPE_DIM
    return pl.pallas_call(
        _pack_w_in_kernel,
        grid=(depth, d // tc),
        in_specs=[pl.BlockSpec((None, n, tc), lambda l, i: (l, 0, i))],
        out_specs=pl.BlockSpec((None, n_out, tc), lambda l, i: (l, 0, i)),
        out_shape=jax.ShapeDtypeStruct((depth, n_out, d), BF16),
        compiler_params=_params("parallel", "parallel"),
        name="pack_w_in",
    )(jnp.swapaxes(w_in, 1, 2))


def _softmax_pv(parts):
    mx = parts[0][0].max(axis=-1, keepdims=True)
    for s, _ in parts[1:]:
        mx = jnp.maximum(mx, s.max(axis=-1, keepdims=True))
    acc = None
    den = None
    for s, v in parts:
        e = jnp.exp(s - mx)
        den_i = e.sum(axis=-1, keepdims=True)
        o_i = _dot(e.astype(BF16), v)
        acc = o_i if acc is None else acc + o_i
        den = den_i if den is None else den + den_i
    return acc * (1.0 / den)


NA_ROWS = 4
NA_UNION = NA_ROWS + NA_KH


def _na_union_start(r, n_rows):
    if isinstance(r, int):
        return min(min(max(r - NA_KH // 2, 0), n_rows - NA_KH), n_rows - NA_UNION)
    return jnp.minimum(jnp.clip(r - NA_KH // 2, 0, n_rows - NA_KH), n_rows - NA_UNION)


def _na_kernel(q_ref, k_ref, v_ref, kc_ref, vc_ref, bias_ref, o_ref, *, n_rows):
    u0 = _na_union_start(pl.program_id(1) * NA_ROWS, n_rows)
    k0 = pl.multiple_of(u0 * GRID_W, GRID_W)
    n_loc = NA_UNION * GRID_W
    c = NA_SCALE * LOG2E
    cs = [slice(h * HEAD_DIM, (h + 1) * HEAD_DIM) for h in range(N_HEADS_NA)]
    def scores(h):
        return (_dot_nt(q_ref[:, cs[h]], k_ref[pl.ds(k0, n_loc), cs[h]]) * c + bias_ref[h],
                _dot_nt(q_ref[:, cs[h]], kc_ref[:, cs[h]]) * c)

    nxt = scores(0)
    for h in range(N_HEADS_NA):
        s_loc, s_ctx = nxt
        if h + 1 < N_HEADS_NA:
            nxt = scores(h + 1)
        m = jnp.maximum(s_loc.max(axis=-1, keepdims=True), s_ctx.max(axis=-1, keepdims=True))
        e_loc = jnp.exp2(s_loc - m)
        e_ctx = jnp.exp2(s_ctx - m)
        den = e_loc.sum(axis=-1, keepdims=True) + e_ctx.sum(axis=-1, keepdims=True)
        o = _dot(e_loc.astype(BF16), v_ref[pl.ds(k0, n_loc), cs[h]]) + _dot(e_ctx.astype(BF16), vc_ref[:, cs[h]])
        o_ref[:, cs[h]] = (o * (1.0 / den)).astype(BF16)


def _na_bias(rpb, n_rows):
    assert n_rows >= NA_UNION and n_rows % NA_ROWS == 0
    h, n_ro, _ = rpb.shape
    w = GRID_W
    col = jnp.arange(w)
    c0 = jnp.clip(col - NA_KW // 2, 0, w - NA_KW)
    col_ok = (col[None, :] >= c0[:, None]) & (col[None, :] < c0[:, None] + NA_KW)
    edge = w - NA_KW
    rpb = rpb.astype(F32)
    text = jnp.concatenate([jnp.broadcast_to(rpb[..., :1], (h, n_ro, edge)), rpb,
                            jnp.broadcast_to(rpb[..., -1:], (h, n_ro, edge)),
                            jnp.zeros((h, n_ro, 1), F32)], axis=-1)
    skew = jnp.tile(text, (1, 1, w))[..., :w * (2 * w - 1)].reshape(h, n_ro, w, 2 * w - 1)
    t = jnp.where(col_ok[None, None], skew[..., w - 1:], NEG_INF) * LOG2E
    masked = jnp.full((h, w, w), NEG_INF * LOG2E, F32)
    kinds = []
    for r in (0, NA_ROWS, n_rows - NA_ROWS):
        u0 = _na_union_start(r, n_rows)
        q_rows = []
        for qr in range(r, r + NA_ROWS):
            r0 = min(max(qr - NA_KH // 2, 0), n_rows - NA_KH)
            tiles = [t[:, kr - qr + NA_KH - 1] if r0 <= kr < r0 + NA_KH else masked
                     for kr in range(u0, u0 + NA_UNION)]
            q_rows.append(jnp.concatenate(tiles, axis=-1))
        kinds.append(jnp.concatenate(q_rows, axis=1))
    return jnp.stack(kinds, axis=0)


def _na_attn(px, pc, bias, cols, *, batch, seq, ctx):
    n_rows = seq // GRID_W
    steps = n_rows // NA_ROWS
    tq = NA_ROWS * GRID_W
    qb, kb, vb = cols.na_q // NA_W, cols.na_k // NA_W, cols.na_v // NA_W
    kind = lambda i: jnp.where(i == 0, 0, jnp.where(i == steps - 1, 2, 1))
    return pl.pallas_call(
        functools.partial(_na_kernel, n_rows=n_rows),
        grid=(batch, steps),
        in_specs=[
            pl.BlockSpec((tq, NA_W), lambda b, i: (b * steps + i, qb)),
            pl.BlockSpec((seq, NA_W), lambda b, i: (b, kb)),
            pl.BlockSpec((seq, NA_W), lambda b, i: (b, vb)),
            pl.BlockSpec((ctx, NA_W), lambda b, i: (b, kb)),
            pl.BlockSpec((ctx, NA_W), lambda b, i: (b, vb)),
            pl.BlockSpec((None,) + bias.shape[1:], lambda b, i: (kind(i), 0, 0, 0)),
        ],
        out_specs=pl.BlockSpec((tq, NA_W), lambda b, i: (b * steps + i, 0)),
        out_shape=jax.ShapeDtypeStruct((batch * seq, NA_W), BF16),
        compiler_params=_params("parallel", "arbitrary"),
        name="na_attn",
    )(px, px, px, pc, pc, bias)


def _ctx_na_kernel(q_ref, k_ref, v_ref, o_ref):
    for h in range(N_HEADS_NA):
        cs = slice(h * HEAD_DIM, (h + 1) * HEAD_DIM)
        s = _dot_nt(q_ref[:, cs], k_ref[:, cs]) * NA_SCALE
        o_ref[:, cs] = _softmax_pv([(s, v_ref[:, cs])]).astype(BF16)


def _ctx_na_attn(pc, cols, *, batch, ctx):
    qb, kb, vb = cols.na_q // NA_W, cols.na_k // NA_W, cols.na_v // NA_W
    return pl.pallas_call(
        _ctx_na_kernel,
        grid=(batch,),
        in_specs=[
            pl.BlockSpec((ctx, NA_W), lambda b: (b, qb)),
            pl.BlockSpec((ctx, NA_W), lambda b: (b, kb)),
            pl.BlockSpec((ctx, NA_W), lambda b: (b, vb)),
        ],
        out_specs=pl.BlockSpec((ctx, NA_W), lambda b: (b, 0)),
        out_shape=jax.ShapeDtypeStruct((batch * ctx, NA_W), BF16),
        compiler_params=_params("parallel"),
        name="ctx_na_attn",
    )(pc, pc, pc)


VT_ROWS = 144
ATTN_CHUNK = 256


def _attn_pipeline_step(qs, k_scr, vt_scr, park, parked):
    n_sets = park[0].shape[0] if park is not None else parked[0].shape[0]
    n_keys = k_scr.shape[0]
    sizes = ([n_keys % ATTN_CHUNK] if n_keys % ATTN_CHUNK else []) + [ATTN_CHUNK] * (n_keys // ATTN_CHUNK)
    run = [None] * n_sets
    acc = [None] * n_sets
    start = 0
    for ch in sizes:
        ks = slice(start, start + ch)
        start += ch
        for t in range(n_sets):
            if park is not None:
                s = _dot_nt(k_scr[ks, :], qs[t])
                park[0][t, ks, :] = s
                m = s.reshape(ch // 8, 8, s.shape[1]).max(axis=0)
                run[t] = m if run[t] is None else jnp.maximum(run[t], m)
            if parked is not None:
                sp = parked[0][t, ks, :]
                e = jnp.exp2(sp.reshape(ch // 8, 8, sp.shape[1]) - parked[1][t][None]).reshape(sp.shape)
                pv = _dot(vt_scr[:, ks], e.astype(BF16))
                acc[t] = pv if acc[t] is None else acc[t] + pv
    if park is not None:
        for t in range(n_sets):
            park[1][t] = jnp.broadcast_to(run[t].max(axis=0, keepdims=True), run[t].shape)
    return acc


def _attn_fill_vt(vt_scr, col0, v):
    n = v.shape[0]
    vt_scr[0:HEAD_DIM, col0:col0 + n] = v.T.astype(BF16)
    vt_scr[HEAD_DIM:, col0:col0 + n] = jnp.ones((VT_ROWS - HEAD_DIM, n), BF16)


def _attn_pipeline(step, last, bufs, body):
    @pl.when(step == 0)
    def _():
        body(bufs[0], None)

    if last > 1:
        @pl.when((step > 0) & (step < last) & (step % 2 == 0))
        def _():
            body(bufs[0], bufs[1])

        @pl.when((step < last) & (step % 2 == 1))
        def _():
            body(bufs[1], bufs[0])

    @pl.when(step == last)
    def _():
        body(None, bufs[(last - 1) % 2])


def _mla_q_kernel(*refs, latent):
    if latent:
        cq_ref, cos_ref, sin_ref, qn_ref, wuq_ref, o_ref = refs
    else:
        cq_ref, qn_ref, wuq_ref, o_ref = refs
    n = _rms(cq_ref[...].astype(F32), qn_ref[...]).astype(BF16)
    for h in range(N_HEADS_MLA):
        q = _dot(n, wuq_ref[h])
        q_pe = q[:, MLA_NOPE_DIM:]
        if latent:
            q_pe = _rope(q_pe, cos_ref[...], sin_ref[...])
        q = jnp.concatenate([q[:, :MLA_NOPE_DIM], q_pe], axis=1) * (MLA_SCALE * LOG2E)
        o_ref[:, h * 2 * LANES:(h + 1) * 2 * LANES] = q.astype(BF16)


def _mla_q(pq, cos, sin, qn, wuq, cols, *, n_q, latent):
    n = pq.shape[0]
    tm = _pick(n_q, (512, 256, 128))
    tiles = n_q // tm
    cqb = cols.cq // MLA_Q_RANK
    in_specs = [pl.BlockSpec((tm, MLA_Q_RANK), lambda i: (i, cqb))]
    args = [pq]
    if latent:
        in_specs += [pl.BlockSpec((tm, LANES), lambda i: (i % tiles, 0))] * 2
        args += [cos, sin]
    in_specs += [pl.BlockSpec((1, MLA_Q_RANK), lambda i: (0, 0)),
                 pl.BlockSpec(wuq.shape, lambda i: (0, 0, 0))]
    args += [qn, wuq]
    return pl.pallas_call(
        functools.partial(_mla_q_kernel, latent=latent),
        grid=(n // tm,),
        in_specs=in_specs,
        out_specs=pl.BlockSpec((tm, N_HEADS_MLA * 2 * LANES), lambda i: (i, 0)),
        out_shape=jax.ShapeDtypeStruct((n, N_HEADS_MLA * 2 * LANES), BF16),
        compiler_params=_params("parallel"),
        name="mla_q" if latent else "ctx_mla_q",
    )(*args)


def _mla_kernel(*refs, ctx, seq, latent, last):
    if latent:
        (q_ref, ckvc_ref, kpec_ref, ckvl_ref, kpel_ref, cosk_ref, sink_ref,
         kvn_ref, wukv_ref, o_ref, k_scr, v_scr, s0, m0, s1, m1) = refs
    else:
        q_ref, ckvc_ref, kpec_ref, kvn_ref, wukv_ref, o_ref, k_scr, v_scr, s0, m0, s1, m1 = refs
    step = pl.program_id(2)

    @pl.when(step == 0)
    def _():
        w = wukv_ref[...]

        def fill(row0, n, ckv_ref, kpe):
            kv = _dot(_rms(ckv_ref[...].astype(F32), kvn_ref[...]).astype(BF16), w)
            k_scr[row0:row0 + n, 0:MLA_NOPE_DIM] = kv[:, :MLA_NOPE_DIM].astype(BF16)
            k_scr[row0:row0 + n, MLA_NOPE_DIM:] = kpe.astype(BF16)
            _attn_fill_vt(v_scr, row0, kv[:, MLA_NOPE_DIM:])

        fill(0, ctx, ckvc_ref, kpec_ref[...])
        if latent:
            fill(ctx, seq, ckvl_ref, _rope(kpel_ref[...].astype(F32), cosk_ref[...], sink_ref[...]))

    def body(park, parked):
        qs = [q_ref[...]] if park is not None else None
        acc = _attn_pipeline_step(qs, k_scr, v_scr, park, parked)
        if parked is not None:
            o_t = acc[0][:MLA_V_DIM] * (1.0 / acc[0][MLA_V_DIM:MLA_V_DIM + 1])
            o_ref[...] = o_t.T.astype(BF16)

    _attn_pipeline(step, last, ((s0, m0), (s1, m1)), body)


def _mla_attn(q, pc, px, cos, sin, kvn, wukv, cols, *, batch, n_q, ctx, seq, latent):
    tq = _pick(n_q, (256, 128))
    nq = n_q // tq
    ckvb, kpeb = cols.ckv // MLA_KV_RANK, cols.kpe // LANES
    n_keys = ctx + (seq if latent else 0)
    q_tile = lambda i: jnp.minimum(i, nq - 1)
    o_tile = lambda i: jnp.maximum(i - 1, 0)
    in_specs = [
        pl.BlockSpec((tq, 2 * LANES), lambda b, h, i: (b * nq + q_tile(i), h)),
        pl.BlockSpec((ctx, MLA_KV_RANK), lambda b, h, i: (b, ckvb)),
        pl.BlockSpec((ctx, LANES), lambda b, h, i: (b, kpeb)),
    ]
    args = [q, pc, pc]
    if latent:
        in_specs += [
            pl.BlockSpec((seq, MLA_KV_RANK), lambda b, h, i: (b, ckvb)),
            pl.BlockSpec((seq, LANES), lambda b, h, i: (b, kpeb)),
            pl.BlockSpec((seq, LANES), lambda b, h, i: (0, 0)),
            pl.BlockSpec((seq, LANES), lambda b, h, i: (0, 0)),
        ]
        args += [px, px, cos, sin]
    in_specs += [
        pl.BlockSpec((1, MLA_KV_RANK), lambda b, h, i: (0, 0)),
        pl.BlockSpec((None, MLA_KV_RANK, 2 * LANES), lambda b, h, i: (h, 0, 0)),
    ]
    args += [kvn, wukv]
    return pl.pallas_call(
        functools.partial(_mla_kernel, ctx=ctx, seq=seq, latent=latent, last=nq),
        grid=(batch, N_HEADS_MLA, nq + 1),
        in_specs=in_specs,
        out_specs=pl.BlockSpec((tq, MLA_V_DIM), lambda b, h, i: (b * nq + o_tile(i), h)),
        out_shape=jax.ShapeDtypeStruct((batch * n_q, MLA_W), BF16),
        scratch_shapes=[pltpu.VMEM((n_keys, 2 * LANES), BF16), pltpu.VMEM((VT_ROWS, n_keys), BF16)]
        + 2 * [pltpu.VMEM((1, n_keys, tq), F32), pltpu.VMEM((1, 8, tq), F32)],
        compiler_params=_params("parallel", "parallel", "arbitrary"),
        name="mla_attn" if latent else "ctx_mla_attn",
    )(*args)


def _diff_kernel(*refs, ctx, seq, latent, lambda_init, last):
    if latent:
        (q_ref, kc_ref, vc_ref, kl_ref, vl_ref, cosq_ref, sinq_ref, cosk_ref, sink_ref,
         lam_ref, sub_ref, o_ref, k_scr, v_scr, s0, m0, s1, m1) = refs
    else:
        q_ref, kc_ref, vc_ref, lam_ref, sub_ref, o_ref, k_scr, v_scr, s0, m0, s1, m1 = refs
    step = pl.program_id(2)

    @pl.when(step == 0)
    def _():
        k_scr[0:ctx, :] = kc_ref[...]
        _attn_fill_vt(v_scr, 0, vc_ref[...].astype(F32))
        if latent:
            k_scr[ctx:ctx + seq, :] = _rope(kl_ref[...].astype(F32), cosk_ref[...], sink_ref[...]).astype(BF16)
            _attn_fill_vt(v_scr, ctx, vl_ref[...].astype(F32))

    def body(park, parked):
        qs = None
        if park is not None:
            q = q_ref[...].astype(F32)
            if latent:
                q = _rope(q, cosq_ref[...], sinq_ref[...])
            q = q * (DIFF_SCALE * LOG2E)
            lane = lax.broadcasted_iota(jnp.int32, q.shape, 1)
            qs = [jnp.where(lane < DIFF_QK_DIM, q, 0.0).astype(BF16),
                  jnp.where(lane >= DIFF_QK_DIM, q, 0.0).astype(BF16)]
        acc = _attn_pipeline_step(qs, k_scr, v_scr, park, parked)
        if parked is not None:
            a1, a2 = acc
            dl = lam_ref[...]
            lam = (jnp.exp(jnp.sum(dl[0:1] * dl[1:2], axis=-1, keepdims=True))
                   - jnp.exp(jnp.sum(dl[2:3] * dl[3:4], axis=-1, keepdims=True)) + lambda_init)
            o_t = (a1[:HEAD_DIM] * (1.0 / a1[HEAD_DIM:HEAD_DIM + 1])
                   - a2[:HEAD_DIM] * (lam / a2[HEAD_DIM:HEAD_DIM + 1]))
            o_ref[...] = (_rms(o_t.T, sub_ref[...]) * (1.0 - lambda_init)).astype(BF16)

    _attn_pipeline(step, last, ((s0, m0), (s1, m1)), body)


def _diff_attn(pq, pc, px, cos, sin, dlam, subln, cols, *, batch, n_q, ctx, seq, latent, lambda_init):
    tq = _pick(n_q, (512, 256, 128))
    nq = n_q // tq
    qb, kb, vb = cols.dq // HEAD_DIM, cols.dk // HEAD_DIM, cols.dv // HEAD_DIM
    n_keys = ctx + (seq if latent else 0)
    q_tile = lambda i: jnp.minimum(i, nq - 1)
    o_tile = lambda i: jnp.maximum(i - 1, 0)
    in_specs = [
        pl.BlockSpec((tq, HEAD_DIM), lambda b, h, i: (b * nq + q_tile(i), qb + h)),
        pl.BlockSpec((ctx, HEAD_DIM), lambda b, h, i: (b, kb + h)),
        pl.BlockSpec((ctx, HEAD_DIM), lambda b, h, i: (b, vb + h)),
    ]
    args = [pq, pc, pc]
    if latent:
        in_specs += [
            pl.BlockSpec((seq, HEAD_DIM), lambda b, h, i: (b, kb + h)),
            pl.BlockSpec((seq, HEAD_DIM), lambda b, h, i: (b, vb + h)),
            pl.BlockSpec((tq, LANES), lambda b, h, i: (q_tile(i), 0)),
            pl.BlockSpec((tq, LANES), lambda b, h, i: (q_tile(i), 0)),
            pl.BlockSpec((seq, LANES), lambda b, h, i: (0, 0)),
            pl.BlockSpec((seq, LANES), lambda b, h, i: (0, 0)),
        ]
        args += [px, px, cos, sin, cos, sin]
    in_specs += [
        pl.BlockSpec((4, DIFF_QK_DIM), lambda b, h, i: (0, 0)),
        pl.BlockSpec((1, HEAD_DIM), lambda b, h, i: (0, 0)),
    ]
    args += [dlam, subln]
    return pl.pallas_call(
        functools.partial(_diff_kernel, ctx=ctx, seq=seq, latent=latent, lambda_init=lambda_init, last=nq),
        grid=(batch, N_HEADS_DIFF, nq + 1),
        in_specs=in_specs,
        out_specs=pl.BlockSpec((tq, HEAD_DIM), lambda b, h, i: (b * nq + o_tile(i), h)),
        out_shape=jax.ShapeDtypeStruct((batch * n_q, DIFF_W), BF16),
        scratch_shapes=[pltpu.VMEM((n_keys, HEAD_DIM), BF16), pltpu.VMEM((VT_ROWS, n_keys), BF16)]
        + 2 * [pltpu.VMEM((2, n_keys, tq), F32), pltpu.VMEM((2, 8, tq), F32)],
        compiler_params=_params("parallel", "parallel", "arbitrary"),
        name="diff_attn" if latent else "ctx_diff_attn",
    )(*args)


def _merge_kernel(oa_ref, ob_ref, od_ref, ga_ref, gb_ref, gd_ref, wa_ref, wb_ref, wd_ref, y_ref):
    y = (_sigmoid(ga_ref[...].astype(F32)) * _dot(oa_ref[...], wa_ref[...])
         + _sigmoid(gb_ref[...].astype(F32)) * _dot(ob_ref[...], wb_ref[...])
         + _sigmoid(gd_ref[...].astype(F32)) * _dot(od_ref[...], wd_ref[...]))
    y_ref[...] = y.astype(BF16)


def _merge(o_na, o_mla, o_diff, p, wb_na, wb_mla, wb_diff, cols, *, tm):
    n = o_na.shape[0]
    d = wb_na.shape[1]
    tn = _pick(d, (512, 256, 128))
    g0 = cols.gates // tn
    gs = d // tn
    return pl.pallas_call(
        _merge_kernel,
        grid=(n // tm, d // tn),
        in_specs=[
            pl.BlockSpec((tm, NA_W), lambda i, j: (i, 0)),
            pl.BlockSpec((tm, MLA_W), lambda i, j: (i, 0)),
            pl.BlockSpec((tm, DIFF_W), lambda i, j: (i, 0)),
            pl.BlockSpec((tm, tn), lambda i, j: (i, g0 + j)),
            pl.BlockSpec((tm, tn), lambda i, j: (i, g0 + gs + j)),
            pl.BlockSpec((tm, tn), lambda i, j: (i, g0 + 2 * gs + j)),
            pl.BlockSpec((NA_W, tn), lambda i, j: (0, j)),
            pl.BlockSpec((MLA_W, tn), lambda i, j: (0, j)),
            pl.BlockSpec((DIFF_W, tn), lambda i, j: (0, j)),
        ],
        out_specs=pl.BlockSpec((tm, tn), lambda i, j: (i, j)),
        out_shape=jax.ShapeDtypeStruct((n, d), BF16),
        compiler_params=_params("parallel", "parallel"),
        name="merge",
    )(o_na, o_mla, o_diff, p, p, p, wb_na, wb_mla, wb_diff)


def _outproj_kernel(x_ref, y_ref, mod_ref, w_ref, o_ref):
    o_ref[...] = x_ref[...] + mod_ref[...] * _dot(y_ref[...], w_ref[...].astype(BF16))


def _outproj(x, y, gate, w_out, *, l, rows_per_group, tm):
    n, d = x.shape
    tn = _pick(d, (512, 256, 128))
    tiles_per_group = rows_per_group // tm
    return pl.pallas_call(
        _outproj_kernel,
        grid=(n // tm, d // tn),
        in_specs=[
            pl.BlockSpec((tm, tn), lambda i, j: (i, j)),
            pl.BlockSpec((tm, d), lambda i, j: (i, 0)),
            pl.BlockSpec((None, 1, tn), lambda i, j: (i // tiles_per_group, 0, j)),
            pl.BlockSpec((None, d, tn), lambda i, j: (l, 0, j)),
        ],
        out_specs=pl.BlockSpec((tm, tn), lambda i, j: (i, j)),
        out_shape=jax.ShapeDtypeStruct((n, d), F32),
        compiler_params=_params("parallel", "parallel"),
        name="outproj",
    )(x, y, gate, w_out)


def _rope_tables(seq):
    quarter = MLA_ROPE_DIM // 4
    freqs = ROPE_THETA ** (-jnp.arange(quarter, dtype=F32) / quarter)
    t = jnp.arange(seq)
    rows, cols = t // GRID_W, t % GRID_W

    def unit(pos):
        ang = pos.astype(F32)[:, None] * freqs
        c, s = jnp.cos(ang), jnp.sin(ang)
        return jnp.concatenate([c, c], axis=1), jnp.concatenate([-s, s], axis=1)

    cr, sr = unit(rows)
    cc, sc = unit(cols)
    cos = jnp.concatenate([cr, cc], axis=1)
    sin = jnp.concatenate([sr, sc], axis=1)
    reps = LANES // MLA_ROPE_DIM
    return jnp.tile(cos, (1, reps)), jnp.tile(sin, (1, reps))


def _pack_heads(w, n_heads, width):
    k = w.shape[0]
    per = w.shape[1] // n_heads
    w = jnp.moveaxis(w.reshape(k, n_heads, per), 1, 0)
    if per < width:
        w = jnp.concatenate([w, jnp.zeros((n_heads, k, width - per), w.dtype)], axis=2)
    return w.astype(BF16)


def kernel(x, c, ctx, c_ctx, w_ada, b_ada, norm_w, ffn_w_in, ffn_w_out, w_in, na_rpb, mla_q_norm, mla_kv_norm,
           mla_w_uq, mla_w_ukv, diff_lambda, diff_subln, w_branch, w_out, final_norm):
    batch, seq, d = x.shape
    n_ctx = ctx.shape[1]
    depth = w_ada.shape[0]
    cols = _Cols(d)
    tm_x = _pick(seq, (1024, 512, 256, 128))
    tm_c = batch * n_ctx

    xs = x.reshape(batch * seq, d)
    xc = ctx.reshape(batch * n_ctx, d)
    cvec = jnp.concatenate([c, c_ctx[None, :], jnp.zeros((8 - batch - 1, d), F32)], axis=0)
    mods = _ada(cvec, w_ada, b_ada).reshape(depth, 8, N_MOD, d)
    cos, sin = _rope_tables(seq)
    wp = _pack_w_in(w_in)

    for l in range(depth):
        last = l == depth - 1
        lambda_init = 0.8 - 0.6 * math.exp(-0.3 * l)
        m = mods[l, :batch]
        mc = mods[l, batch:batch + 1]
        nw = norm_w[l].reshape(3, 1, d)

        xs = _ffn(xs, m, nw[0], ffn_w_in, ffn_w_out, l=l, s=0, mod0=0, rows_per_group=seq, tm=tm_x)
        xc = _ffn(xc, mc, nw[0], ffn_w_in, ffn_w_out, l=l, s=0, mod0=0, rows_per_group=tm_c, tm=tm_c)

        px = _inproj(xs, m, nw[1], wp, l=l, rows_per_group=seq, tm=tm_x)
        pc = _inproj(xc, mc, nw[1], wp, l=l, rows_per_group=tm_c, tm=tm_c)

        bias = _na_bias(na_rpb[l], seq // GRID_W)
        qn = mla_q_norm[l].reshape(1, MLA_Q_RANK)
        kvn = mla_kv_norm[l].reshape(1, MLA_KV_RANK)
        wuq = _pack_heads(mla_w_uq[l], N_HEADS_MLA, 2 * LANES)
        wukv = _pack_heads(mla_w_ukv[l], N_HEADS_MLA, 2 * LANES)
        subln = diff_subln[l].reshape(1, HEAD_DIM)
        wb = w_branch[l].astype(BF16)
        wb_na, wb_mla, wb_diff = wb[:NA_W], wb[NA_W:NA_W + MLA_W], wb[NA_W + MLA_W:]

        o_na = _na_attn(px, pc, bias, cols, batch=batch, seq=seq, ctx=n_ctx)
        q_mla = _mla_q(px, cos, sin, qn, wuq, cols, n_q=seq, latent=True)
        o_mla = _mla_attn(q_mla, pc, px, cos, sin, kvn, wukv, cols,
                          batch=batch, n_q=seq, ctx=n_ctx, seq=seq, latent=True)
        o_diff = _diff_attn(px, pc, px, cos, sin, diff_lambda[l], subln, cols,
                            batch=batch, n_q=seq, ctx=n_ctx, seq=seq, latent=True, lambda_init=lambda_init)
        y = _merge(o_na, o_mla, o_diff, px, wb_na, wb_mla, wb_diff, cols, tm=tm_x)
        xs = _outproj(xs, y, m[:, 5:6], w_out, l=l, rows_per_group=seq, tm=tm_x)
        xs = _ffn(xs, m, nw[2], ffn_w_in, ffn_w_out, l=l, s=1, mod0=6, rows_per_group=seq, tm=tm_x)

        if not last:
            oc_na = _ctx_na_attn(pc, cols, batch=batch, ctx=n_ctx)
            qc_mla = _mla_q(pc, cos, sin, qn, wuq, cols, n_q=n_ctx, latent=False)
            oc_mla = _mla_attn(qc_mla, pc, px, cos, sin, kvn, wukv, cols,
                               batch=batch, n_q=n_ctx, ctx=n_ctx, seq=seq, latent=False)
            oc_diff = _diff_attn(pc, pc, px, cos, sin, diff_lambda[l], subln, cols,
                                 batch=batch, n_q=n_ctx, ctx=n_ctx, seq=seq, latent=False, lambda_init=lambda_init)
            yc = _merge(oc_na, oc_mla, oc_diff, pc, wb_na, wb_mla, wb_diff, cols, tm=tm_c)
            xc = _outproj(xc, yc, mc[:, 5:6], w_out, l=l, rows_per_group=tm_c, tm=tm_c)
            xc = _ffn(xc, mc, nw[2], ffn_w_in, ffn_w_out, l=l, s=1, mod0=6, rows_per_group=tm_c, tm=tm_c)

    return _final_norm(xs, final_norm.reshape(1, d), tm=tm_x).reshape(batch, seq, d)
```

```python
import functools
import math

import jax
import jax.numpy as jnp
from jax import lax
from jax.experimental import pallas as pl
from jax.experimental.pallas import tpu as pltpu

GRID_W = 64
HEAD_DIM = 128
N_HEADS_NA = 6
N_HEADS_MLA = 5
N_HEADS_DIFF = 5
NA_KH = 8
NA_KW = 16
MLA_Q_RANK = 768
MLA_KV_RANK = 512
MLA_NOPE_DIM = 128
MLA_ROPE_DIM = 64
MLA_V_DIM = 128
DIFF_QK_DIM = HEAD_DIM // 2
N_MOD = 9
ROPE_THETA = 10000.0
NORM_EPS = 1e-6
NEG_INF = -1e30
NA_W = N_HEADS_NA * HEAD_DIM
MLA_W = N_HEADS_MLA * MLA_V_DIM
DIFF_W = N_HEADS_DIFF * HEAD_DIM
NA_SCALE = HEAD_DIM ** -0.5
MLA_SCALE = (MLA_NOPE_DIM + MLA_ROPE_DIM) ** -0.5
DIFF_SCALE = DIFF_QK_DIM ** -0.5

LOG2E = 1.4426950408889634
LANES = 128
VMEM_LIMIT = 60 * 1024 * 1024
BF16 = jnp.bfloat16
F32 = jnp.float32


def _params(*sem):
    return pltpu.CompilerParams(dimension_semantics=sem, vmem_limit_bytes=VMEM_LIMIT)


def _pick(n, prefs):
    for p in prefs:
        if n % p == 0:
            return p
    return n


def _dot(a, b):
    return jnp.dot(a, b, preferred_element_type=F32)


def _dot_nt(a, b):
    return lax.dot_general(a, b, (((1,), (1,)), ((), ())), preferred_element_type=F32)


def _rms(x, g):
    return x * lax.rsqrt(jnp.mean(x * x, axis=-1, keepdims=True) + NORM_EPS) * g


def _silu(x):
    return x * (1.0 / (1.0 + jnp.exp(-x)))


def _sigmoid(x):
    return 1.0 / (1.0 + jnp.exp(-x))


def _swap16(x):
    lane = lax.broadcasted_iota(jnp.int32, x.shape, x.ndim - 1)
    down = pltpu.roll(x, 16, x.ndim - 1)
    up = pltpu.roll(x, LANES - 16, x.ndim - 1)
    return jnp.where((lane & 16) != 0, down, up)


def _rope(x, cos, sin):
    return x * cos + _swap16(x) * sin


NORM_ROWS = 16


def _modulated_norm(h_ref, x_ref, nw, shift, scale):
    gain = nw * (1.0 + scale)

    def group(i, carry):
        r0 = pl.multiple_of(i * NORM_ROWS, NORM_ROWS)
        x = x_ref[pl.ds(r0, NORM_ROWS), :]
        rstd = lax.rsqrt(jnp.mean(x * x, axis=-1, keepdims=True) + NORM_EPS)
        h_ref[pl.ds(r0, NORM_ROWS), :] = (x * rstd * gain + shift).astype(BF16)
        return carry

    lax.fori_loop(0, x_ref.shape[0] // NORM_ROWS, group, 0, unroll=8)


def _ada_kernel(c_ref, w_ref, b_ref, o_ref):
    a = _silu(c_ref[...]).astype(BF16)
    o_ref[...] = _dot(a, w_ref[...].astype(BF16)) + b_ref[...]


def _ada(cvec, w_ada, b_ada):
    depth, d, n = w_ada.shape
    tn = _pick(n, (1024, 512, 256, 128))
    return pl.pallas_call(
        _ada_kernel,
        grid=(depth, n // tn),
        in_specs=[
            pl.BlockSpec((8, d), lambda l, j: (0, 0)),
            pl.BlockSpec((None, d, tn), lambda l, j: (l, 0, j)),
            pl.BlockSpec((None, 1, tn), lambda l, j: (l, 0, j)),
        ],
        out_specs=pl.BlockSpec((None, 8, tn), lambda l, j: (l, 0, j)),
        out_shape=jax.ShapeDtypeStruct((depth, 8, n), F32),
        compiler_params=_params("parallel", "parallel"),
        name="ada",
    )(cvec, w_ada, b_ada.reshape(depth, 1, n))


def _ffn_in_kernel(x_ref, mod_ref, nw_ref, wa_ref, wb_ref, g_ref, h_ref, *, mod0):
    @pl.when(pl.program_id(1) == 0)
    def _():
        _modulated_norm(h_ref, x_ref, nw_ref[...], mod_ref[mod0:mod0 + 1, :], mod_ref[mod0 + 1:mod0 + 2, :])

    h = h_ref[...]
    a = _dot(h, wa_ref[...].astype(BF16))
    b = _dot(h, wb_ref[...].astype(BF16))
    g_ref[...] = (_silu(a) * b).astype(BF16)


def _ffn_out_kernel(x_ref, g_ref, mod_ref, w_ref, o_ref, *, mod0):
    gate = 0.5 * mod_ref[mod0 + 2:mod0 + 3, :]
    o_ref[...] = x_ref[...] + gate * _dot(g_ref[...], w_ref[...].astype(BF16))


def _ffn(x, mod, nw, w_in, w_out, *, l, s, mod0, rows_per_group, tm):
    n, d = x.shape
    f = w_out.shape[2]
    fc = _pick(f, (512, 256, 128))
    nf = f // fc
    tn = _pick(d, (256, 128))
    tiles_per_group = rows_per_group // tm
    g = pl.pallas_call(
        functools.partial(_ffn_in_kernel, mod0=mod0),
        grid=(n // tm, nf),
        in_specs=[
            pl.BlockSpec((tm, d), lambda i, j: (i, 0)),
            pl.BlockSpec((None, N_MOD, d), lambda i, j: (i // tiles_per_group, 0, 0)),
            pl.BlockSpec((1, d), lambda i, j: (0, 0)),
            pl.BlockSpec((None, None, d, fc), lambda i, j: (l, s, 0, j)),
            pl.BlockSpec((None, None, d, fc), lambda i, j: (l, s, 0, nf + j)),
        ],
        out_specs=pl.BlockSpec((tm, fc), lambda i, j: (i, j)),
        out_shape=jax.ShapeDtypeStruct((n, f), BF16),
        scratch_shapes=[pltpu.VMEM((tm, d), BF16)],
        compiler_params=_params("parallel", "arbitrary"),
        name="ffn_in",
    )(x, mod, nw, w_in, w_in)
    return pl.pallas_call(
        functools.partial(_ffn_out_kernel, mod0=mod0),
        grid=(n // tm, d // tn),
        in_specs=[
            pl.BlockSpec((tm, tn), lambda i, j: (i, j)),
            pl.BlockSpec((tm, f), lambda i, j: (i, 0)),
            pl.BlockSpec((None, N_MOD, tn), lambda i, j: (i // tiles_per_group, 0, j)),
            pl.BlockSpec((None, None, f, tn), lambda i, j: (l, s, 0, j)),
        ],
        out_specs=pl.BlockSpec((tm, tn), lambda i, j: (i, j)),
        out_shape=jax.ShapeDtypeStruct((n, d), F32),
        compiler_params=_params("parallel", "parallel"),
        name="ffn_out",
    )(x, g, mod, w_out)


def _final_norm_kernel(x_ref, g_ref, o_ref):
    o_ref[...] = _rms(x_ref[...], g_ref[...])


def _final_norm(x, g, *, tm):
    n, d = x.shape
    return pl.pallas_call(
        _final_norm_kernel,
        grid=(n // tm,),
        in_specs=[pl.BlockSpec((tm, d), lambda i: (i, 0)), pl.BlockSpec((1, d), lambda i: (0, 0))],
        out_specs=pl.BlockSpec((tm, d), lambda i: (i, 0)),
        out_shape=jax.ShapeDtypeStruct((n, d), F32),
        compiler_params=_params("parallel"),
        name="final_norm",
    )(x, g)


def _inproj_kernel(x_ref, mod_ref, nw_ref, w_ref, o_ref, h_ref):
    @pl.when(pl.program_id(1) == 0)
    def _():
        _modulated_norm(h_ref, x_ref, nw_ref[...], mod_ref[3:4, :], mod_ref[4:5, :])

    o_ref[...] = _dot_nt(h_ref[...], w_ref[...]).astype(BF16)


def _inproj(x, mod, nw, w, *, l, rows_per_group, tm):
    n, d = x.shape
    nw_cols = w.shape[1]
    tn = _pick(nw_cols, (512, 256, 128))
    tiles_per_group = rows_per_group // tm
    return pl.pallas_call(
        _inproj_kernel,
        grid=(n // tm, nw_cols // tn),
        in_specs=[
            pl.BlockSpec((tm, d), lambda i, j: (i, 0)),
            pl.BlockSpec((None, N_MOD, d), lambda i, j: (i // tiles_per_group, 0, 0)),
            pl.BlockSpec((1, d), lambda i, j: (0, 0)),
            pl.BlockSpec((None, tn, d), lambda i, j: (l, j, 0)),
        ],
        out_specs=pl.BlockSpec((tm, tn), lambda i, j: (i, j)),
        out_shape=jax.ShapeDtypeStruct((n, nw_cols), BF16),
        scratch_shapes=[pltpu.VMEM((tm, d), BF16)],
        compiler_params=_params("parallel", "arbitrary"),
        name="inproj",
    )(x, mod, nw, w)


class _Cols:
    def __init__(self, d):
        self.na_q = 0
        self.na_k = NA_W
        self.na_v = 2 * NA_W
        self.cq = 3 * NA_W
        self.ckv = self.cq + MLA_Q_RANK
        self.gates = self.ckv + MLA_KV_RANK
        self.dq = self.gates + 3 * d
        self.dk = self.dq + DIFF_W
        self.dv = self.dk + DIFF_W
        self.kpe = self.dv + DIFF_W
        self.total = self.kpe + LANES


def _pack_w_in_kernel(w_ref, o_ref):
    w = w_ref[...]
    o = 3 * NA_W + MLA_Q_RANK + MLA_KV_RANK
    o2 = o + MLA_ROPE_DIM
    o3 = o2 + 3 * DIFF_W
    pad = jnp.zeros((LANES - MLA_ROPE_DIM, w.shape[1]), w.dtype)
    o_ref[...] = jnp.concatenate([w[:o], w[o3:], w[o2:o3], w[o:o2], pad], axis=0).astype(BF16)


def _pack_w_in(w_in):
    depth, d, n = w_in.shape
    tc = _pick(d, (256, 128))
    n_out = n + LANES - MLA_ROPE_DIM
    return pl.pallas_call(
        _pack_w_in_kernel,
        grid=(depth, d // tc),
        in_specs=[pl.BlockSpec((None, n, tc), lambda l, i: (l, 0, i))],
        out_specs=pl.BlockSpec((None, n_out, tc), lambda l, i: (l, 0, i)),
        out_shape=jax.ShapeDtypeStruct((depth, n_out, d), BF16),
        compiler_params=_params("parallel", "parallel"),
        name="pack_w_in",
    )(jnp.swapaxes(w_in, 1, 2))


def _softmax_pv(parts):
    mx = parts[0][0].max(axis=-1, keepdims=True)
    for s, _ in parts[1:]:
        mx = jnp.maximum(mx, s.max(axis=-1, keepdims=True))
    acc = None
    den = None
    for s, v in parts:
        e = jnp.exp(s - mx)
        den_i = e.sum(axis=-1, keepdims=True)
        o_i = _dot(e.astype(BF16), v)
        acc = o_i if acc is None else acc + o_i
        den = den_i if den is None else den + den_i
    return acc * (1.0 / den)


NA_ROWS = 4
NA_UNION = NA_ROWS + NA_KH


def _na_union_start(r, n_rows):
    if isinstance(r, int):
        return min(min(max(r - NA_KH // 2, 0), n_rows - NA_KH), n_rows - NA_UNION)
    return jnp.minimum(jnp.clip(r - NA_KH // 2, 0, n_rows - NA_KH), n_rows - NA_UNION)


def _na_kernel(q_ref, k_ref, v_ref, kc_ref, vc_ref, bias_ref, o_ref, *, n_rows):
    u0 = _na_union_start(pl.program_id(1) * NA_ROWS, n_rows)
    k0 = pl.multiple_of(u0 * GRID_W, GRID_W)
    n_loc = NA_UNION * GRID_W
    c = NA_SCALE * LOG2E
    cs = [slice(h * HEAD_DIM, (h + 1) * HEAD_DIM) for h in range(N_HEADS_NA)]
    def scores(h):
        return (_dot_nt(q_ref[:, cs[h]], k_ref[pl.ds(k0, n_loc), cs[h]]) * c + bias_ref[h],
                _dot_nt(q_ref[:, cs[h]], kc_ref[:, cs[h]]) * c)

    nxt = scores(0)
    for h in range(N_HEADS_NA):
        s_loc, s_ctx = nxt
        if h + 1 < N_HEADS_NA:
            nxt = scores(h + 1)
        m = jnp.maximum(s_loc.max(axis=-1, keepdims=True), s_ctx.max(axis=-1, keepdims=True))
        e_loc = jnp.exp2(s_loc - m)
        e_ctx = jnp.exp2(s_ctx - m)
        den = e_loc.sum(axis=-1, keepdims=True) + e_ctx.sum(axis=-1, keepdims=True)
        o = _dot(e_loc.astype(BF16), v_ref[pl.ds(k0, n_loc), cs[h]]) + _dot(e_ctx.astype(BF16), vc_ref[:, cs[h]])
        o_ref[:, cs[h]] = (o * (1.0 / den)).astype(BF16)


def _na_bias(rpb, n_rows):
    assert n_rows >= NA_UNION and n_rows % NA_ROWS == 0
    h, n_ro, _ = rpb.shape
    w = GRID_W
    col = jnp.arange(w)
    c0 = jnp.clip(col - NA_KW // 2, 0, w - NA_KW)
    col_ok = (col[None, :] >= c0[:, None]) & (col[None, :] < c0[:, None] + NA_KW)
    edge = w - NA_KW
    rpb = rpb.astype(F32)
    text = jnp.concatenate([jnp.broadcast_to(rpb[..., :1], (h, n_ro, edge)), rpb,
                            jnp.broadcast_to(rpb[..., -1:], (h, n_ro, edge)),
                            jnp.zeros((h, n_ro, 1), F32)], axis=-1)
    skew = jnp.tile(text, (1, 1, w))[..., :w * (2 * w - 1)].reshape(h, n_ro, w, 2 * w - 1)
    t = jnp.where(col_ok[None, None], skew[..., w - 1:], NEG_INF) * LOG2E
    masked = jnp.full((h, w, w), NEG_INF * LOG2E, F32)
    kinds = []
    for r in (0, NA_ROWS, n_rows - NA_ROWS):
        u0 = _na_union_start(r, n_rows)
        q_rows = []
        for qr in range(r, r + NA_ROWS):
            r0 = min(max(qr - NA_KH // 2, 0), n_rows - NA_KH)
            tiles = [t[:, kr - qr + NA_KH - 1] if r0 <= kr < r0 + NA_KH else masked
                     for kr in range(u0, u0 + NA_UNION)]
            q_rows.append(jnp.concatenate(tiles, axis=-1))
        kinds.append(jnp.concatenate(q_rows, axis=1))
    return jnp.stack(kinds, axis=0)


def _na_attn(px, pc, bias, cols, *, batch, seq, ctx):
    n_rows = seq // GRID_W
    steps = n_rows // NA_ROWS
    tq = NA_ROWS * GRID_W
    qb, kb, vb = cols.na_q // NA_W, cols.na_k // NA_W, cols.na_v // NA_W
    kind = lambda i: jnp.where(i == 0, 0, jnp.where(i == steps - 1, 2, 1))
    return pl.pallas_call(
        functools.partial(_na_kernel, n_rows=n_rows),
        grid=(batch, steps),
        in_specs=[
            pl.BlockSpec((tq, NA_W), lambda b, i: (b * steps + i, qb)),
            pl.BlockSpec((seq, NA_W), lambda b, i: (b, kb)),
            pl.BlockSpec((seq, NA_W), lambda b, i: (b, vb)),
            pl.BlockSpec((ctx, NA_W), lambda b, i: (b, kb)),
            pl.BlockSpec((ctx, NA_W), lambda b, i: (b, vb)),
            pl.BlockSpec((None,) + bias.shape[1:], lambda b, i: (kind(i), 0, 0, 0)),
        ],
        out_specs=pl.BlockSpec((tq, NA_W), lambda b, i: (b * steps + i, 0)),
        out_shape=jax.ShapeDtypeStruct((batch * seq, NA_W), BF16),
        compiler_params=_params("parallel", "arbitrary"),
        name="na_attn",
    )(px, px, px, pc, pc, bias)


def _ctx_na_kernel(q_ref, k_ref, v_ref, o_ref):
    for h in range(N_HEADS_NA):
        cs = slice(h * HEAD_DIM, (h + 1) * HEAD_DIM)
        s = _dot_nt(q_ref[:, cs], k_ref[:, cs]) * NA_SCALE
        o_ref[:, cs] = _softmax_pv([(s, v_ref[:, cs])]).astype(BF16)


def _ctx_na_attn(pc, cols, *, batch, ctx):
    qb, kb, vb = cols.na_q // NA_W, cols.na_k // NA_W, cols.na_v // NA_W
    return pl.pallas_call(
        _ctx_na_kernel,
        grid=(batch,),
        in_specs=[
            pl.BlockSpec((ctx, NA_W), lambda b: (b, qb)),
            pl.BlockSpec((ctx, NA_W), lambda b: (b, kb)),
            pl.BlockSpec((ctx, NA_W), lambda b: (b, vb)),
        ],
        out_specs=pl.BlockSpec((ctx, NA_W), lambda b: (b, 0)),
        out_shape=jax.ShapeDtypeStruct((batch * ctx, NA_W), BF16),
        compiler_params=_params("parallel"),
        name="ctx_na_attn",
    )(pc, pc, pc)


VT_ROWS = 144
ATTN_CHUNK = 256


def _attn_pipeline_step(qs, k_scr, vt_scr, park, parked):
    n_sets = park[0].shape[0] if park is not None else parked[0].shape[0]
    n_keys = k_scr.shape[0]
    sizes = ([n_keys % ATTN_CHUNK] if n_keys % ATTN_CHUNK else []) + [ATTN_CHUNK] * (n_keys // ATTN_CHUNK)
    run = [None] * n_sets
    acc = [None] * n_sets
    start = 0
    for ch in sizes:
        ks = slice(start, start + ch)
        start += ch
        for t in range(n_sets):
            if park is not None:
                s = _dot_nt(k_scr[ks, :], qs[t])
                park[0][t, ks, :] = s
                m = s.reshape(ch // 8, 8, s.shape[1]).max(axis=0)
                run[t] = m if run[t] is None else jnp.maximum(run[t], m)
            if parked is not None:
                sp = parked[0][t, ks, :]
                e = jnp.exp2(sp.reshape(ch // 8, 8, sp.shape[1]) - parked[1][t][None]).reshape(sp.shape)
                pv = _dot(vt_scr[:, ks], e.astype(BF16))
                acc[t] = pv if acc[t] is None else acc[t] + pv
    if park is not None:
        for t in range(n_sets):
            park[1][t] = jnp.broadcast_to(run[t].max(axis=0, keepdims=True), run[t].shape)
    return acc


def _attn_fill_vt(vt_scr, col0, v):
    n = v.shape[0]
    vt_scr[0:HEAD_DIM, col0:col0 + n] = v.T.astype(BF16)
    vt_scr[HEAD_DIM:, col0:col0 + n] = jnp.ones((VT_ROWS - HEAD_DIM, n), BF16)


def _attn_pipeline(step, last, bufs, body):
    @pl.when(step == 0)
    def _():
        body(bufs[0], None)

    if last > 1:
        @pl.when((step > 0) & (step < last) & (step % 2 == 0))
        def _():
            body(bufs[0], bufs[1])

        @pl.when((step < last) & (step % 2 == 1))
        def _():
            body(bufs[1], bufs[0])

    @pl.when(step == last)
    def _():
        body(None, bufs[(last - 1) % 2])


def _mla_q_kernel(*refs, latent):
    if latent:
        cq_ref, cos_ref, sin_ref, qn_ref, wuq_ref, o_ref = refs
    else:
        cq_ref, qn_ref, wuq_ref, o_ref = refs
    n = _rms(cq_ref[...].astype(F32), qn_ref[...]).astype(BF16)
    for h in range(N_HEADS_MLA):
        q = _dot(n, wuq_ref[h])
        q_pe = q[:, MLA_NOPE_DIM:]
        if latent:
            q_pe = _rope(q_pe, cos_ref[...], sin_ref[...])
        q = jnp.concatenate([q[:, :MLA_NOPE_DIM], q_pe], axis=1) * (MLA_SCALE * LOG2E)
        o_ref[:, h * 2 * LANES:(h + 1) * 2 * LANES] = q.astype(BF16)


def _mla_q(pq, cos, sin, qn, wuq, cols, *, n_q, latent):
    n = pq.shape[0]
    tm = _pick(n_q, (512, 256, 128))
    tiles = n_q // tm
    cqb = cols.cq // MLA_Q_RANK
    in_specs = [pl.BlockSpec((tm, MLA_Q_RANK), lambda i: (i, cqb))]
    args = [pq]
    if latent:
        in_specs += [pl.BlockSpec((tm, LANES), lambda i: (i % tiles, 0))] * 2
        args += [cos, sin]
    in_specs += [pl.BlockSpec((1, MLA_Q_RANK), lambda i: (0, 0)),
                 pl.BlockSpec(wuq.shape, lambda i: (0, 0, 0))]
    args += [qn, wuq]
    return pl.pallas_call(
        functools.partial(_mla_q_kernel, latent=latent),
        grid=(n // tm,),
        in_specs=in_specs,
        out_specs=pl.BlockSpec((tm, N_HEADS_MLA * 2 * LANES), lambda i: (i, 0)),
        out_shape=jax.ShapeDtypeStruct((n, N_HEADS_MLA * 2 * LANES), BF16),
        compiler_params=_params("parallel"),
        name="mla_q" if latent else "ctx_mla_q",
    )(*args)


def _mla_kernel(*refs, ctx, seq, latent, last):
    if latent:
        (q_ref, ckvc_ref, kpec_ref, ckvl_ref, kpel_ref, cosk_ref, sink_ref,
         kvn_ref, wukv_ref, o_ref, k_scr, v_scr, s0, m0, s1, m1) = refs
    else:
        q_ref, ckvc_ref, kpec_ref, kvn_ref, wukv_ref, o_ref, k_scr, v_scr, s0, m0, s1, m1 = refs
    step = pl.program_id(2)

    @pl.when(step == 0)
    def _():
        w = wukv_ref[...]

        def fill(row0, n, ckv_ref, kpe):
            kv = _dot(_rms(ckv_ref[...].astype(F32), kvn_ref[...]).astype(BF16), w)
            k_scr[row0:row0 + n, 0:MLA_NOPE_DIM] = kv[:, :MLA_NOPE_DIM].astype(BF16)
            k_scr[row0:row0 + n, MLA_NOPE_DIM:] = kpe.astype(BF16)
            _attn_fill_vt(v_scr, row0, kv[:, MLA_NOPE_DIM:])

        fill(0, ctx, ckvc_ref, kpec_ref[...])
        if latent:
            fill(ctx, seq, ckvl_ref, _rope(kpel_ref[...].astype(F32), cosk_ref[...], sink_ref[...]))

    def body(park, parked):
        qs = [q_ref[...]] if park is not None else None
        acc = _attn_pipeline_step(qs, k_scr, v_scr, park, parked)
        if parked is not None:
            o_t = acc[0][:MLA_V_DIM] * (1.0 / acc[0][MLA_V_DIM:MLA_V_DIM + 1])
            o_ref[...] = o_t.T.astype(BF16)

    _attn_pipeline(step, last, ((s0, m0), (s1, m1)), body)


def _mla_attn(q, pc, px, cos, sin, kvn, wukv, cols, *, batch, n_q, ctx, seq, latent):
    tq = _pick(n_q, (256, 128))
    nq = n_q // tq
    ckvb, kpeb = cols.ckv // MLA_KV_RANK, cols.kpe // LANES
    n_keys = ctx + (seq if latent else 0)
    q_tile = lambda i: jnp.minimum(i, nq - 1)
    o_tile = lambda i: jnp.maximum(i - 1, 0)
    in_specs = [
        pl.BlockSpec((tq, 2 * LANES), lambda b, h, i: (b * nq + q_tile(i), h)),
        pl.BlockSpec((ctx, MLA_KV_RANK), lambda b, h, i: (b, ckvb)),
        pl.BlockSpec((ctx, LANES), lambda b, h, i: (b, kpeb)),
    ]
    args = [q, pc, pc]
    if latent:
        in_specs += [
            pl.BlockSpec((seq, MLA_KV_RANK), lambda b, h, i: (b, ckvb)),
            pl.BlockSpec((seq, LANES), lambda b, h, i: (b, kpeb)),
            pl.BlockSpec((seq, LANES), lambda b, h, i: (0, 0)),
            pl.BlockSpec((seq, LANES), lambda b, h, i: (0, 0)),
        ]
        args += [px, px, cos, sin]
    in_specs += [
        pl.BlockSpec((1, MLA_KV_RANK), lambda b, h, i: (0, 0)),
        pl.BlockSpec((None, MLA_KV_RANK, 2 * LANES), lambda b, h, i: (h, 0, 0)),
    ]
    args += [kvn, wukv]
    return pl.pallas_call(
        functools.partial(_mla_kernel, ctx=ctx, seq=seq, latent=latent, last=nq),
        grid=(batch, N_HEADS_MLA, nq + 1),
        in_specs=in_specs,
        out_specs=pl.BlockSpec((tq, MLA_V_DIM), lambda b, h, i: (b * nq + o_tile(i), h)),
        out_shape=jax.ShapeDtypeStruct((batch * n_q, MLA_W), BF16),
        scratch_shapes=[pltpu.VMEM((n_keys, 2 * LANES), BF16), pltpu.VMEM((VT_ROWS, n_keys), BF16)]
        + 2 * [pltpu.VMEM((1, n_keys, tq), F32), pltpu.VMEM((1, 8, tq), F32)],
        compiler_params=_params("parallel", "parallel", "arbitrary"),
        name="mla_attn" if latent else "ctx_mla_attn",
    )(*args)


def _diff_kernel(*refs, ctx, seq, latent, lambda_init, last):
    if latent:
        (q_ref, kc_ref, vc_ref, kl_ref, vl_ref, cosq_ref, sinq_ref, cosk_ref, sink_ref,
         lam_ref, sub_ref, o_ref, k_scr, v_scr, s0, m0, s1, m1) = refs
    else:
        q_ref, kc_ref, vc_ref, lam_ref, sub_ref, o_ref, k_scr, v_scr, s0, m0, s1, m1 = refs
    step = pl.program_id(2)

    @pl.when(step == 0)
    def _():
        k_scr[0:ctx, :] = kc_ref[...]
        _attn_fill_vt(v_scr, 0, vc_ref[...].astype(F32))
        if latent:
            k_scr[ctx:ctx + seq, :] = _rope(kl_ref[...].astype(F32), cosk_ref[...], sink_ref[...]).astype(BF16)
            _attn_fill_vt(v_scr, ctx, vl_ref[...].astype(F32))

    def body(park, parked):
        qs = None
        if park is not None:
            q = q_ref[...].astype(F32)
            if latent:
                q = _rope(q, cosq_ref[...], sinq_ref[...])
            q = q * (DIFF_SCALE * LOG2E)
            lane = lax.broadcasted_iota(jnp.int32, q.shape, 1)
            qs = [jnp.where(lane < DIFF_QK_DIM, q, 0.0).astype(BF16),
                  jnp.where(lane >= DIFF_QK_DIM, q, 0.0).astype(BF16)]
        acc = _attn_pipeline_step(qs, k_scr, v_scr, park, parked)
        if parked is not None:
            a1, a2 = acc
            dl = lam_ref[...]
            lam = (jnp.exp(jnp.sum(dl[0:1] * dl[1:2], axis=-1, keepdims=True))
                   - jnp.exp(jnp.sum(dl[2:3] * dl[3:4], axis=-1, keepdims=True)) + lambda_init)
            o_t = (a1[:HEAD_DIM] * (1.0 / a1[HEAD_DIM:HEAD_DIM + 1])
                   - a2[:HEAD_DIM] * (lam / a2[HEAD_DIM:HEAD_DIM + 1]))
            o_ref[...] = (_rms(o_t.T, sub_ref[...]) * (1.0 - lambda_init)).astype(BF16)

    _attn_pipeline(step, last, ((s0, m0), (s1, m1)), body)


def _diff_attn(pq, pc, px, cos, sin, dlam, subln, cols, *, batch, n_q, ctx, seq, latent, lambda_init):
    tq = _pick(n_q, (512, 256, 128))
    nq = n_q // tq
    qb, kb, vb = cols.dq // HEAD_DIM, cols.dk // HEAD_DIM, cols.dv // HEAD_DIM
    n_keys = ctx + (seq if latent else 0)
    q_tile = lambda i: jnp.minimum(i, nq - 1)
    o_tile = lambda i: jnp.maximum(i - 1, 0)
    in_specs = [
        pl.BlockSpec((tq, HEAD_DIM), lambda b, h, i: (b * nq + q_tile(i), qb + h)),
        pl.BlockSpec((ctx, HEAD_DIM), lambda b, h, i: (b, kb + h)),
        pl.BlockSpec((ctx, HEAD_DIM), lambda b, h, i: (b, vb + h)),
    ]
    args = [pq, pc, pc]
    if latent:
        in_specs += [
            pl.BlockSpec((seq, HEAD_DIM), lambda b, h, i: (b, kb + h)),
            pl.BlockSpec((seq, HEAD_DIM), lambda b, h, i: (b, vb + h)),
            pl.BlockSpec((tq, LANES), lambda b, h, i: (q_tile(i), 0)),
            pl.BlockSpec((tq, LANES), lambda b, h, i: (q_tile(i), 0)),
            pl.BlockSpec((seq, LANES), lambda b, h, i: (0, 0)),
            pl.BlockSpec((seq, LANES), lambda b, h, i: (0, 0)),
        ]
        args += [px, px, cos, sin, cos, sin]
    in_specs += [
        pl.BlockSpec((4, DIFF_QK_DIM), lambda b, h, i: (0, 0)),
        pl.BlockSpec((1, HEAD_DIM), lambda b, h, i: (0, 0)),
    ]
    args += [dlam, subln]
    return pl.pallas_call(
        functools.partial(_diff_kernel, ctx=ctx, seq=seq, latent=latent, lambda_init=lambda_init, last=nq),
        grid=(batch, N_HEADS_DIFF, nq + 1),
        in_specs=in_specs,
        out_specs=pl.BlockSpec((tq, HEAD_DIM), lambda b, h, i: (b * nq + o_tile(i), h)),
        out_shape=jax.ShapeDtypeStruct((batch * n_q, DIFF_W), BF16),
        scratch_shapes=[pltpu.VMEM((n_keys, HEAD_DIM), BF16), pltpu.VMEM((VT_ROWS, n_keys), BF16)]
        + 2 * [pltpu.VMEM((2, n_keys, tq), F32), pltpu.VMEM((2, 8, tq), F32)],
        compiler_params=_params("parallel", "parallel", "arbitrary"),
        name="diff_attn" if latent else "ctx_diff_attn",
    )(*args)


def _merge_kernel(oa_ref, ob_ref, od_ref, ga_ref, gb_ref, gd_ref, wa_ref, wb_ref, wd_ref, y_ref):
    y = (_sigmoid(ga_ref[...].astype(F32)) * _dot(oa_ref[...], wa_ref[...])
         + _sigmoid(gb_ref[...].astype(F32)) * _dot(ob_ref[...], wb_ref[...])
         + _sigmoid(gd_ref[...].astype(F32)) * _dot(od_ref[...], wd_ref[...]))
    y_ref[...] = y.astype(BF16)


def _merge(o_na, o_mla, o_diff, p, wb_na, wb_mla, wb_diff, cols, *, tm):
    n = o_na.shape[0]
    d = wb_na.shape[1]
    tn = _pick(d, (512, 256, 128))
    g0 = cols.gates // tn
    gs = d // tn
    return pl.pallas_call(
        _merge_kernel,
        grid=(n // tm, d // tn),
        in_specs=[
            pl.BlockSpec((tm, NA_W), lambda i, j: (i, 0)),
            pl.BlockSpec((tm, MLA_W), lambda i, j: (i, 0)),
            pl.BlockSpec((tm, DIFF_W), lambda i, j: (i, 0)),
            pl.BlockSpec((tm, tn), lambda i, j: (i, g0 + j)),
            pl.BlockSpec((tm, tn), lambda i, j: (i, g0 + gs + j)),
            pl.BlockSpec((tm, tn), lambda i, j: (i, g0 + 2 * gs + j)),
            pl.BlockSpec((NA_W, tn), lambda i, j: (0, j)),
            pl.BlockSpec((MLA_W, tn), lambda i, j: (0, j)),
            pl.BlockSpec((DIFF_W, tn), lambda i, j: (0, j)),
        ],
        out_specs=pl.BlockSpec((tm, tn), lambda i, j: (i, j)),
        out_shape=jax.ShapeDtypeStruct((n, d), BF16),
        compiler_params=_params("parallel", "parallel"),
        name="merge",
    )(o_na, o_mla, o_diff, p, p, p, wb_na, wb_mla, wb_diff)


def _outproj_kernel(x_ref, y_ref, mod_ref, w_ref, o_ref):
    o_ref[...] = x_ref[...] + mod_ref[...] * _dot(y_ref[...], w_ref[...].astype(BF16))


def _outproj(x, y, gate, w_out, *, l, rows_per_group, tm):
    n, d = x.shape
    tn = _pick(d, (512, 256, 128))
    tiles_per_group = rows_per_group // tm
    return pl.pallas_call(
        _outproj_kernel,
        grid=(n // tm, d // tn),
        in_specs=[
            pl.BlockSpec((tm, tn), lambda i, j: (i, j)),
            pl.BlockSpec((tm, d), lambda i, j: (i, 0)),
            pl.BlockSpec((None, 1, tn), lambda i, j: (i // tiles_per_group, 0, j)),
            pl.BlockSpec((None, d, tn), lambda i, j: (l, 0, j)),
        ],
        out_specs=pl.BlockSpec((tm, tn), lambda i, j: (i, j)),
        out_shape=jax.ShapeDtypeStruct((n, d), F32),
        compiler_params=_params("parallel", "parallel"),
        name="outproj",
    )(x, y, gate, w_out)


def _rope_tables(seq):
    quarter = MLA_ROPE_DIM // 4
    freqs = ROPE_THETA ** (-jnp.arange(quarter, dtype=F32) / quarter)
    t = jnp.arange(seq)
    rows, cols = t // GRID_W, t % GRID_W

    def unit(pos):
        ang = pos.astype(F32)[:, None] * freqs
        c, s = jnp.cos(ang), jnp.sin(ang)
        return jnp.concatenate([c, c], axis=1), jnp.concatenate([-s, s], axis=1)

    cr, sr = unit(rows)
    cc, sc = unit(cols)
    cos = jnp.concatenate([cr, cc], axis=1)
    sin = jnp.concatenate([sr, sc], axis=1)
    reps = LANES // MLA_ROPE_DIM
    return jnp.tile(cos, (1, reps)), jnp.tile(sin, (1, reps))


def _pack_heads(w, n_heads, width):
    k = w.shape[0]
    per = w.shape[1] // n_heads
    w = jnp.moveaxis(w.reshape(k, n_heads, per), 1, 0)
    if per < width:
        w = jnp.concatenate([w, jnp.zeros((n_heads, k, width - per), w.dtype)], axis=2)
    return w.astype(BF16)


def kernel(x, c, ctx, c_ctx, w_ada, b_ada, norm_w, ffn_w_in, ffn_w_out, w_in, na_rpb, mla_q_norm, mla_kv_norm,
           mla_w_uq, mla_w_ukv, diff_lambda, diff_subln, w_branch, w_out, final_norm):
    batch, seq, d = x.shape
    n_ctx = ctx.shape[1]
    depth = w_ada.shape[0]
    cols = _Cols(d)
    tm_x = _pick(seq, (1024, 512, 256, 128))
    tm_wide = _pick(seq, (2048, 1024, 512, 256, 128))
    tm_c = batch * n_ctx

    xs = x.reshape(batch * seq, d)
    xc = ctx.reshape(batch * n_ctx, d)
    cvec = jnp.concatenate([c, c_ctx[None, :], jnp.zeros((8 - batch - 1, d), F32)], axis=0)
    mods = _ada(cvec, w_ada, b_ada).reshape(depth, 8, N_MOD, d)
    cos, sin = _rope_tables(seq)
    wp = _pack_w_in(w_in)

    for l in range(depth):
        last = l == depth - 1
        lambda_init = 0.8 - 0.6 * math.exp(-0.3 * l)
        m = mods[l, :batch]
        mc = mods[l, batch:batch + 1]
        nw = norm_w[l].reshape(3, 1, d)

        xs = _ffn(xs, m, nw[0], ffn_w_in, ffn_w_out, l=l, s=0, mod0=0, rows_per_group=seq, tm=tm_x)
        xc = _ffn(xc, mc, nw[0], ffn_w_in, ffn_w_out, l=l, s=0, mod0=0, rows_per_group=tm_c, tm=tm_c)

        px = _inproj(xs, m, nw[1], wp, l=l, rows_per_group=seq, tm=tm_wide)
        pc = _inproj(xc, mc, nw[1], wp, l=l, rows_per_group=tm_c, tm=tm_c)

        bias = _na_bias(na_rpb[l], seq // GRID_W)
        qn = mla_q_norm[l].reshape(1, MLA_Q_RANK)
        kvn = mla_kv_norm[l].reshape(1, MLA_KV_RANK)
        wuq = _pack_heads(mla_w_uq[l], N_HEADS_MLA, 2 * LANES)
        wukv = _pack_heads(mla_w_ukv[l], N_HEADS_MLA, 2 * LANES)
        subln = diff_subln[l].reshape(1, HEAD_DIM)
        wb = w_branch[l].astype(BF16)
        wb_na, wb_mla, wb_diff = wb[:NA_W], wb[NA_W:NA_W + MLA_W], wb[NA_W + MLA_W:]

        o_na = _na_attn(px, pc, bias, cols, batch=batch, seq=seq, ctx=n_ctx)
        q_mla = _mla_q(px, cos, sin, qn, wuq, cols, n_q=seq, latent=True)
        o_mla = _mla_attn(q_mla, pc, px, cos, sin, kvn, wukv, cols,
                          batch=batch, n_q=seq, ctx=n_ctx, seq=seq, latent=True)
        o_diff = _diff_attn(px, pc, px, cos, sin, diff_lambda[l], subln, cols,
                            batch=batch, n_q=seq, ctx=n_ctx, seq=seq, latent=True, lambda_init=lambda_init)
        y = _merge(o_na, o_mla, o_diff, px, wb_na, wb_mla, wb_diff, cols, tm=tm_x)
        xs = _outproj(xs, y, m[:, 5:6], w_out, l=l, rows_per_group=seq, tm=tm_wide)
        xs = _ffn(xs, m, nw[2], ffn_w_in, ffn_w_out, l=l, s=1, mod0=6, rows_per_group=seq, tm=tm_x)

        if not last:
            oc_na = _ctx_na_attn(pc, cols, batch=batch, ctx=n_ctx)
            qc_mla = _mla_q(pc, cos, sin, qn, wuq, cols, n_q=n_ctx, latent=False)
            oc_mla = _mla_attn(qc_mla, pc, px, cos, sin, kvn, wukv, cols,
                               batch=batch, n_q=n_ctx, ctx=n_ctx, seq=seq, latent=False)
            oc_diff = _diff_attn(pc, pc, px, cos, sin, diff_lambda[l], subln, cols,
                                 batch=batch, n_q=n_ctx, ctx=n_ctx, seq=seq, latent=False, lambda_init=lambda_init)
            yc = _merge(oc_na, oc_mla, oc_diff, pc, wb_na, wb_mla, wb_diff, cols, tm=tm_c)
            xc = _outproj(xc, yc, mc[:, 5:6], w_out, l=l, rows_per_group=tm_c, tm=tm_c)
            xc = _ffn(xc, mc, nw[2], ffn_w_in, ffn_w_out, l=l, s=1, mod0=6, rows_per_group=tm_c, tm=tm_c)

    return _final_norm(xs, final_norm.reshape(1, d), tm=tm_x).reshape(batch, seq, d)
```

```python
import functools
import math

import jax
import jax.numpy as jnp
from jax import lax
from jax.experimental import pallas as pl
from jax.experimental.pallas import tpu as pltpu

GRID_W = 64
HEAD_DIM = 128
N_HEADS_NA = 6
N_HEADS_MLA = 5
N_HEADS_DIFF = 5
NA_KH = 8
NA_KW = 16
MLA_Q_RANK = 768
MLA_KV_RANK = 512
MLA_NOPE_DIM = 128
MLA_ROPE_DIM = 64
MLA_V_DIM = 128
DIFF_QK_DIM = HEAD_DIM // 2
N_MOD = 9
ROPE_THETA = 10000.0
NORM_EPS = 1e-6
NEG_INF = -1e30
NA_W = N_HEADS_NA * HEAD_DIM
MLA_W = N_HEADS_MLA * MLA_V_DIM
DIFF_W = N_HEADS_DIFF * HEAD_DIM
NA_SCALE = HEAD_DIM ** -0.5
MLA_SCALE = (MLA_NOPE_DIM + MLA_ROPE_DIM) ** -0.5
DIFF_SCALE = DIFF_QK_DIM ** -0.5

LOG2E = 1.4426950408889634
LANES = 128
VMEM_LIMIT = 60 * 1024 * 1024
BF16 = jnp.bfloat16
F32 = jnp.float32


def _params(*sem):
    return pltpu.CompilerParams(dimension_semantics=sem, vmem_limit_bytes=VMEM_LIMIT)


def _pick(n, prefs):
    for p in prefs:
        if n % p == 0:
            return p
    return n


def _dot(a, b):
    return jnp.dot(a, b, preferred_element_type=F32)


def _dot_nt(a, b):
    return lax.dot_general(a, b, (((1,), (1,)), ((), ())), preferred_element_type=F32)


def _rms(x, g):
    return x * lax.rsqrt(jnp.mean(x * x, axis=-1, keepdims=True) + NORM_EPS) * g


def _silu(x):
    return x * (1.0 / (1.0 + jnp.exp(-x)))


def _sigmoid(x):
    return 1.0 / (1.0 + jnp.exp(-x))


def _swap16(x):
    lane = lax.broadcasted_iota(jnp.int32, x.shape, x.ndim - 1)
    down = pltpu.roll(x, 16, x.ndim - 1)
    up = pltpu.roll(x, LANES - 16, x.ndim - 1)
    return jnp.where((lane & 16) != 0, down, up)


def _rope(x, cos, sin):
    return x * cos + _swap16(x) * sin


NORM_ROWS = 16


def _modulated_norm(h_ref, x_ref, nw, shift, scale):
    gain = nw * (1.0 + scale)

    def group(i, carry):
        r0 = pl.multiple_of(i * NORM_ROWS, NORM_ROWS)
        x = x_ref[pl.ds(r0, NORM_ROWS), :]
        rstd = lax.rsqrt(jnp.mean(x * x, axis=-1, keepdims=True) + NORM_EPS)
        h_ref[pl.ds(r0, NORM_ROWS), :] = (x * rstd * gain + shift).astype(BF16)
        return carry

    lax.fori_loop(0, x_ref.shape[0] // NORM_ROWS, group, 0, unroll=8)


def _ada_kernel(c_ref, w_ref, b_ref, o_ref):
    a = _silu(c_ref[...]).astype(BF16)
    o_ref[...] = _dot(a, w_ref[...].astype(BF16)) + b_ref[...]


def _ada(cvec, w_ada, b_ada):
    depth, d, n = w_ada.shape
    tn = _pick(n, (1024, 512, 256, 128))
    return pl.pallas_call(
        _ada_kernel,
        grid=(depth, n // tn),
        in_specs=[
            pl.BlockSpec((8, d), lambda l, j: (0, 0)),
            pl.BlockSpec((None, d, tn), lambda l, j: (l, 0, j)),
            pl.BlockSpec((None, 1, tn), lambda l, j: (l, 0, j)),
        ],
        out_specs=pl.BlockSpec((None, 8, tn), lambda l, j: (l, 0, j)),
        out_shape=jax.ShapeDtypeStruct((depth, 8, n), F32),
        compiler_params=_params("parallel", "parallel"),
        name="ada",
    )(cvec, w_ada, b_ada.reshape(depth, 1, n))


def _ffn_in_kernel(x_ref, mod_ref, nw_ref, wa_ref, wb_ref, g_ref, h_ref, *, mod0):
    @pl.when(pl.program_id(1) == 0)
    def _():
        _modulated_norm(h_ref, x_ref, nw_ref[...], mod_ref[mod0:mod0 + 1, :], mod_ref[mod0 + 1:mod0 + 2, :])

    h = h_ref[...]
    a = _dot(h, wa_ref[...].astype(BF16))
    b = _dot(h, wb_ref[...].astype(BF16))
    g_ref[...] = (_silu(a) * b).astype(BF16)


def _ffn_out_kernel(x_ref, g_ref, mod_ref, w_ref, o_ref, *, mod0):
    gate = 0.5 * mod_ref[mod0 + 2:mod0 + 3, :]
    o_ref[...] = x_ref[...] + gate * _dot(g_ref[...], w_ref[...].astype(BF16))


def _ffn(x, mod, nw, w_in, w_out, *, l, s, mod0, rows_per_group, tm):
    n, d = x.shape
    f = w_out.shape[2]
    fc = _pick(f, (512, 256, 128))
    nf = f // fc
    tn = _pick(d, (256, 128))
    tiles_per_group = rows_per_group // tm
    g = pl.pallas_call(
        functools.partial(_ffn_in_kernel, mod0=mod0),
        grid=(n // tm, nf),
        in_specs=[
            pl.BlockSpec((tm, d), lambda i, j: (i, 0)),
            pl.BlockSpec((None, N_MOD, d), lambda i, j: (i // tiles_per_group, 0, 0)),
            pl.BlockSpec((1, d), lambda i, j: (0, 0)),
            pl.BlockSpec((None, None, d, fc), lambda i, j: (l, s, 0, j)),
            pl.BlockSpec((None, None, d, fc), lambda i, j: (l, s, 0, nf + j)),
        ],
        out_specs=pl.BlockSpec((tm, fc), lambda i, j: (i, j)),
        out_shape=jax.ShapeDtypeStruct((n, f), BF16),
        scratch_shapes=[pltpu.VMEM((tm, d), BF16)],
        compiler_params=_params("parallel", "arbitrary"),
        name="ffn_in",
    )(x, mod, nw, w_in, w_in)
    return pl.pallas_call(
        functools.partial(_ffn_out_kernel, mod0=mod0),
        grid=(n // tm, d // tn),
        in_specs=[
            pl.BlockSpec((tm, tn), lambda i, j: (i, j)),
            pl.BlockSpec((tm, f), lambda i, j: (i, 0)),
            pl.BlockSpec((None, N_MOD, tn), lambda i, j: (i // tiles_per_group, 0, j)),
            pl.BlockSpec((None, None, f, tn), lambda i, j: (l, s, 0, j)),
        ],
        out_specs=pl.BlockSpec((tm, tn), lambda i, j: (i, j)),
        out_shape=jax.ShapeDtypeStruct((n, d), F32),
        compiler_params=_params("parallel", "parallel"),
        name="ffn_out",
    )(x, g, mod, w_out)


def _final_norm_kernel(x_ref, g_ref, o_ref):
    o_ref[...] = _rms(x_ref[...], g_ref[...])


def _final_norm(x, g, *, tm):
    n, d = x.shape
    return pl.pallas_call(
        _final_norm_kernel,
        grid=(n // tm,),
        in_specs=[pl.BlockSpec((tm, d), lambda i: (i, 0)), pl.BlockSpec((1, d), lambda i: (0, 0))],
        out_specs=pl.BlockSpec((tm, d), lambda i: (i, 0)),
        out_shape=jax.ShapeDtypeStruct((n, d), F32),
        compiler_params=_params("parallel"),
        name="final_norm",
    )(x, g)


def _inproj_kernel(x_ref, mod_ref, nw_ref, w_ref, o_ref, h_ref):
    @pl.when(pl.program_id(1) == 0)
    def _():
        _modulated_norm(h_ref, x_ref, nw_ref[...], mod_ref[3:4, :], mod_ref[4:5, :])

    o_ref[...] = _dot_nt(h_ref[...], w_ref[...]).astype(BF16)


def _inproj(x, mod, nw, w, *, l, rows_per_group, tm):
    n, d = x.shape
    nw_cols = w.shape[1]
    tn = _pick(nw_cols, (512, 256, 128))
    tiles_per_group = rows_per_group // tm
    return pl.pallas_call(
        _inproj_kernel,
        grid=(n // tm, nw_cols // tn),
        in_specs=[
            pl.BlockSpec((tm, d), lambda i, j: (i, 0)),
            pl.BlockSpec((None, N_MOD, d), lambda i, j: (i // tiles_per_group, 0, 0)),
            pl.BlockSpec((1, d), lambda i, j: (0, 0)),
            pl.BlockSpec((None, tn, d), lambda i, j: (l, j, 0)),
        ],
        out_specs=pl.BlockSpec((tm, tn), lambda i, j: (i, j)),
        out_shape=jax.ShapeDtypeStruct((n, nw_cols), BF16),
        scratch_shapes=[pltpu.VMEM((tm, d), BF16)],
        compiler_params=_params("parallel", "arbitrary"),
        name="inproj",
    )(x, mod, nw, w)


class _Cols:
    def __init__(self, d):
        self.na_q = 0
        self.na_k = NA_W
        self.na_v = 2 * NA_W
        self.cq = 3 * NA_W
        self.ckv = self.cq + MLA_Q_RANK
        self.gates = self.ckv + MLA_KV_RANK
        self.dq = self.gates + 3 * d
        self.dk = self.dq + DIFF_W
        self.dv = self.dk + DIFF_W
        self.kpe = self.dv + DIFF_W
        self.total = self.kpe + LANES


def _pack_w_in_kernel(w_ref, o_ref):
    w = w_ref[...]
    o = 3 * NA_W + MLA_Q_RANK + MLA_KV_RANK
    o2 = o + MLA_ROPE_DIM
    o3 = o2 + 3 * DIFF_W
    pad = jnp.zeros((LANES - MLA_ROPE_DIM, w.shape[1]), w.dtype)
    o_ref[...] = jnp.concatenate([w[:o], w[o3:], w[o2:o3], w[o:o2], pad], axis=0).astype(BF16)


def _pack_w_in(w_in):
    depth, d, n = w_in.shape
    tc = _pick(d, (256, 128))
    n_out = n + LANES - MLA_ROPE_DIM
    return pl.pallas_call(
        _pack_w_in_kernel,
        grid=(depth, d // tc),
        in_specs=[pl.BlockSpec((None, n, tc), lambda l, i: (l, 0, i))],
        out_specs=pl.BlockSpec((None, n_out, tc), lambda l, i: (l, 0, i)),
        out_shape=jax.ShapeDtypeStruct((depth, n_out, d), BF16),
        compiler_params=_params("parallel", "parallel"),
        name="pack_w_in",
    )(jnp.swapaxes(w_in, 1, 2))


def _softmax_pv(parts):
    mx = parts[0][0].max(axis=-1, keepdims=True)
    for s, _ in parts[1:]:
        mx = jnp.maximum(mx, s.max(axis=-1, keepdims=True))
    acc = None
    den = None
    for s, v in parts:
        e = jnp.exp(s - mx)
        den_i = e.sum(axis=-1, keepdims=True)
        o_i = _dot(e.astype(BF16), v)
        acc = o_i if acc is None else acc + o_i
        den = den_i if den is None else den + den_i
    return acc * (1.0 / den)


NA_ROWS = 4
NA_UNION = NA_ROWS + NA_KH


def _na_union_start(r, n_rows):
    if isinstance(r, int):
        return min(min(max(r - NA_KH // 2, 0), n_rows - NA_KH), n_rows - NA_UNION)
    return jnp.minimum(jnp.clip(r - NA_KH // 2, 0, n_rows - NA_KH), n_rows - NA_UNION)


def _na_kernel(q_ref, k_ref, v_ref, kc_ref, vc_ref, bias_ref, o_ref, *, n_rows):
    u0 = _na_union_start(pl.program_id(1) * NA_ROWS, n_rows)
    k0 = pl.multiple_of(u0 * GRID_W, GRID_W)
    n_loc = NA_UNION * GRID_W
    c = NA_SCALE * LOG2E
    cs = [slice(h * HEAD_DIM, (h + 1) * HEAD_DIM) for h in range(N_HEADS_NA)]
    def scores(h):
        return (_dot_nt(q_ref[:, cs[h]], k_ref[pl.ds(k0, n_loc), cs[h]]) * c + bias_ref[h],
                _dot_nt(q_ref[:, cs[h]], kc_ref[:, cs[h]]) * c)

    nxt = scores(0)
    for h in range(N_HEADS_NA):
        s_loc, s_ctx = nxt
        if h + 1 < N_HEADS_NA:
            nxt = scores(h + 1)
        m = jnp.maximum(s_loc.max(axis=-1, keepdims=True), s_ctx.max(axis=-1, keepdims=True))
        e_loc = jnp.exp2(s_loc - m)
        e_ctx = jnp.exp2(s_ctx - m)
        den = e_loc.sum(axis=-1, keepdims=True) + e_ctx.sum(axis=-1, keepdims=True)
        o = _dot(e_loc.astype(BF16), v_ref[pl.ds(k0, n_loc), cs[h]]) + _dot(e_ctx.astype(BF16), vc_ref[:, cs[h]])
        o_ref[:, cs[h]] = (o * (1.0 / den)).astype(BF16)


def _na_bias(rpb, n_rows):
    assert n_rows >= NA_UNION and n_rows % NA_ROWS == 0
    h, n_ro, _ = rpb.shape
    w = GRID_W
    col = jnp.arange(w)
    c0 = jnp.clip(col - NA_KW // 2, 0, w - NA_KW)
    col_ok = (col[None, :] >= c0[:, None]) & (col[None, :] < c0[:, None] + NA_KW)
    edge = w - NA_KW
    rpb = rpb.astype(F32)
    text = jnp.concatenate([jnp.broadcast_to(rpb[..., :1], (h, n_ro, edge)), rpb,
                            jnp.broadcast_to(rpb[..., -1:], (h, n_ro, edge)),
                            jnp.zeros((h, n_ro, 1), F32)], axis=-1)
    skew = jnp.tile(text, (1, 1, w))[..., :w * (2 * w - 1)].reshape(h, n_ro, w, 2 * w - 1)
    t = jnp.where(col_ok[None, None], skew[..., w - 1:], NEG_INF) * LOG2E
    masked = jnp.full((h, w, w), NEG_INF * LOG2E, F32)
    kinds = []
    for r in (0, NA_ROWS, n_rows - NA_ROWS):
        u0 = _na_union_start(r, n_rows)
        q_rows = []
        for qr in range(r, r + NA_ROWS):
            r0 = min(max(qr - NA_KH // 2, 0), n_rows - NA_KH)
            tiles = [t[:, kr - qr + NA_KH - 1] if r0 <= kr < r0 + NA_KH else masked
                     for kr in range(u0, u0 + NA_UNION)]
            q_rows.append(jnp.concatenate(tiles, axis=-1))
        kinds.append(jnp.concatenate(q_rows, axis=1))
    return jnp.stack(kinds, axis=0)


def _na_attn(px, pc, bias, cols, *, batch, seq, ctx):
    n_rows = seq // GRID_W
    steps = n_rows // NA_ROWS
    tq = NA_ROWS * GRID_W
    qb, kb, vb = cols.na_q // NA_W, cols.na_k // NA_W, cols.na_v // NA_W
    kind = lambda i: jnp.where(i == 0, 0, jnp.where(i == steps - 1, 2, 1))
    return pl.pallas_call(
        functools.partial(_na_kernel, n_rows=n_rows),
        grid=(batch, steps),
        in_specs=[
            pl.BlockSpec((tq, NA_W), lambda b, i: (b * steps + i, qb)),
            pl.BlockSpec((seq, NA_W), lambda b, i: (b, kb)),
            pl.BlockSpec((seq, NA_W), lambda b, i: (b, vb)),
            pl.BlockSpec((ctx, NA_W), lambda b, i: (b, kb)),
            pl.BlockSpec((ctx, NA_W), lambda b, i: (b, vb)),
            pl.BlockSpec((None,) + bias.shape[1:], lambda b, i: (kind(i), 0, 0, 0)),
        ],
        out_specs=pl.BlockSpec((tq, NA_W), lambda b, i: (b * steps + i, 0)),
        out_shape=jax.ShapeDtypeStruct((batch * seq, NA_W), BF16),
        compiler_params=_params("parallel", "arbitrary"),
        name="na_attn",
    )(px, px, px, pc, pc, bias)


def _ctx_na_kernel(q_ref, k_ref, v_ref, o_ref):
    for h in range(N_HEADS_NA):
        cs = slice(h * HEAD_DIM, (h + 1) * HEAD_DIM)
        s = _dot_nt(q_ref[:, cs], k_ref[:, cs]) * NA_SCALE
        o_ref[:, cs] = _softmax_pv([(s, v_ref[:, cs])]).astype(BF16)


def _ctx_na_attn(pc, cols, *, batch, ctx):
    qb, kb, vb = cols.na_q // NA_W, cols.na_k // NA_W, cols.na_v // NA_W
    return pl.pallas_call(
        _ctx_na_kernel,
        grid=(batch,),
        in_specs=[
            pl.BlockSpec((ctx, NA_W), lambda b: (b, qb)),
            pl.BlockSpec((ctx, NA_W), lambda b: (b, kb)),
            pl.BlockSpec((ctx, NA_W), lambda b: (b, vb)),
        ],
        out_specs=pl.BlockSpec((ctx, NA_W), lambda b: (b, 0)),
        out_shape=jax.ShapeDtypeStruct((batch * ctx, NA_W), BF16),
        compiler_params=_params("parallel"),
        name="ctx_na_attn",
    )(pc, pc, pc)


VT_ROWS = 144
ATTN_CHUNK = 256


def _attn_pipeline_step(qs, k_scr, vt_scr, park, parked):
    n_sets = park[0].shape[0] if park is not None else parked[0].shape[0]
    n_keys = k_scr.shape[0]
    sizes = ([n_keys % ATTN_CHUNK] if n_keys % ATTN_CHUNK else []) + [ATTN_CHUNK] * (n_keys // ATTN_CHUNK)
    run = [None] * n_sets
    acc = [None] * n_sets
    start = 0
    for ch in sizes:
        ks = slice(start, start + ch)
        start += ch
        for t in range(n_sets):
            if park is not None:
                s = _dot_nt(k_scr[ks, :], qs[t])
                park[0][t, ks, :] = s
                m = s.reshape(ch // 8, 8, s.shape[1]).max(axis=0)
                run[t] = m if run[t] is None else jnp.maximum(run[t], m)
            if parked is not None:
                sp = parked[0][t, ks, :]
                e = jnp.exp2(sp.reshape(ch // 8, 8, sp.shape[1]) - parked[1][t][None]).reshape(sp.shape)
                pv = _dot(vt_scr[:, ks], e.astype(BF16))
                acc[t] = pv if acc[t] is None else acc[t] + pv
    if park is not None:
        for t in range(n_sets):
            park[1][t] = jnp.broadcast_to(run[t].max(axis=0, keepdims=True), run[t].shape)
    return acc


def _attn_fill_vt(vt_scr, col0, v):
    n = v.shape[0]
    vt_scr[0:HEAD_DIM, col0:col0 + n] = v.T.astype(BF16)
    vt_scr[HEAD_DIM:, col0:col0 + n] = jnp.ones((VT_ROWS - HEAD_DIM, n), BF16)


def _attn_pipeline(step, last, bufs, body):
    @pl.when(step == 0)
    def _():
        body(bufs[0], None)

    if last > 1:
        @pl.when((step > 0) & (step < last) & (step % 2 == 0))
        def _():
            body(bufs[0], bufs[1])

        @pl.when((step < last) & (step % 2 == 1))
        def _():
            body(bufs[1], bufs[0])

    @pl.when(step == last)
    def _():
        body(None, bufs[(last - 1) % 2])


def _mla_q_kernel(*refs, latent):
    if latent:
        cq_ref, cos_ref, sin_ref, qn_ref, wuq_ref, o_ref = refs
    else:
        cq_ref, qn_ref, wuq_ref, o_ref = refs
    n = _rms(cq_ref[...].astype(F32), qn_ref[...]).astype(BF16)
    for h in range(N_HEADS_MLA):
        q = _dot(n, wuq_ref[h])
        q_pe = q[:, MLA_NOPE_DIM:]
        if latent:
            q_pe = _rope(q_pe, cos_ref[...], sin_ref[...])
        q = jnp.concatenate([q[:, :MLA_NOPE_DIM], q_pe], axis=1) * (MLA_SCALE * LOG2E)
        o_ref[:, h * 2 * LANES:(h + 1) * 2 * LANES] = q.astype(BF16)


def _mla_q(pq, cos, sin, qn, wuq, cols, *, n_q, latent):
    n = pq.shape[0]
    tm = _pick(n_q, (512, 256, 128))
    tiles = n_q // tm
    cqb = cols.cq // MLA_Q_RANK
    in_specs = [pl.BlockSpec((tm, MLA_Q_RANK), lambda i: (i, cqb))]
    args = [pq]
    if latent:
        in_specs += [pl.BlockSpec((tm, LANES), lambda i: (i % tiles, 0))] * 2
        args += [cos, sin]
    in_specs += [pl.BlockSpec((1, MLA_Q_RANK), lambda i: (0, 0)),
                 pl.BlockSpec(wuq.shape, lambda i: (0, 0, 0))]
    args += [qn, wuq]
    return pl.pallas_call(
        functools.partial(_mla_q_kernel, latent=latent),
        grid=(n // tm,),
        in_specs=in_specs,
        out_specs=pl.BlockSpec((tm, N_HEADS_MLA * 2 * LANES), lambda i: (i, 0)),
        out_shape=jax.ShapeDtypeStruct((n, N_HEADS_MLA * 2 * LANES), BF16),
        compiler_params=_params("parallel"),
        name="mla_q" if latent else "ctx_mla_q",
    )(*args)


def _mla_kernel(*refs, ctx, seq, latent, last):
    if latent:
        (q_ref, ckvc_ref, kpec_ref, ckvl_ref, kpel_ref, cosk_ref, sink_ref,
         kvn_ref, wukv_ref, o_ref, k_scr, v_scr, s0, m0, s1, m1) = refs
    else:
        q_ref, ckvc_ref, kpec_ref, kvn_ref, wukv_ref, o_ref, k_scr, v_scr, s0, m0, s1, m1 = refs
    step = pl.program_id(2)

    @pl.when(step == 0)
    def _():
        w = wukv_ref[...]

        def fill(row0, n, ckv_ref, kpe):
            kv = _dot(_rms(ckv_ref[...].astype(F32), kvn_ref[...]).astype(BF16), w)
            k_scr[row0:row0 + n, 0:MLA_NOPE_DIM] = kv[:, :MLA_NOPE_DIM].astype(BF16)
            k_scr[row0:row0 + n, MLA_NOPE_DIM:] = kpe.astype(BF16)
            _attn_fill_vt(v_scr, row0, kv[:, MLA_NOPE_DIM:])

        fill(0, ctx, ckvc_ref, kpec_ref[...])
        if latent:
            fill(ctx, seq, ckvl_ref, _rope(kpel_ref[...].astype(F32), cosk_ref[...], sink_ref[...]))

    n_sub, _, tq = s0.shape

    def body(park, parked):
        qs = [q_ref[t * tq:(t + 1) * tq, :] for t in range(n_sub)] if park is not None else None
        acc = _attn_pipeline_step(qs, k_scr, v_scr, park, parked)
        if parked is not None:
            for t in range(n_sub):
                o_t = acc[t][:MLA_V_DIM] * (1.0 / acc[t][MLA_V_DIM:MLA_V_DIM + 1])
                o_ref[t * tq:(t + 1) * tq, :] = o_t.T.astype(BF16)

    _attn_pipeline(step, last, ((s0, m0), (s1, m1)), body)


def _mla_attn(q, pc, px, cos, sin, kvn, wukv, cols, *, batch, n_q, ctx, seq, latent):
    tq = _pick(n_q, (256, 128))
    n_sub = 2 if n_q % (2 * tq) == 0 else 1
    rows = n_sub * tq
    nq = n_q // rows
    ckvb, kpeb = cols.ckv // MLA_KV_RANK, cols.kpe // LANES
    n_keys = ctx + (seq if latent else 0)
    q_tile = lambda i: jnp.minimum(i, nq - 1)
    o_tile = lambda i: jnp.maximum(i - 1, 0)
    in_specs = [
        pl.BlockSpec((rows, 2 * LANES), lambda b, h, i: (b * nq + q_tile(i), h)),
        pl.BlockSpec((ctx, MLA_KV_RANK), lambda b, h, i: (b, ckvb)),
        pl.BlockSpec((ctx, LANES), lambda b, h, i: (b, kpeb)),
    ]
    args = [q, pc, pc]
    if latent:
        in_specs += [
            pl.BlockSpec((seq, MLA_KV_RANK), lambda b, h, i: (b, ckvb)),
            pl.BlockSpec((seq, LANES), lambda b, h, i: (b, kpeb)),
            pl.BlockSpec((seq, LANES), lambda b, h, i: (0, 0)),
            pl.BlockSpec((seq, LANES), lambda b, h, i: (0, 0)),
        ]
        args += [px, px, cos, sin]
    in_specs += [
        pl.BlockSpec((1, MLA_KV_RANK), lambda b, h, i: (0, 0)),
        pl.BlockSpec((None, MLA_KV_RANK, 2 * LANES), lambda b, h, i: (h, 0, 0)),
    ]
    args += [kvn, wukv]
    return pl.pallas_call(
        functools.partial(_mla_kernel, ctx=ctx, seq=seq, latent=latent, last=nq),
        grid=(batch, N_HEADS_MLA, nq + 1),
        in_specs=in_specs,
        out_specs=pl.BlockSpec((rows, MLA_V_DIM), lambda b, h, i: (b * nq + o_tile(i), h)),
        out_shape=jax.ShapeDtypeStruct((batch * n_q, MLA_W), BF16),
        scratch_shapes=[pltpu.VMEM((n_keys, 2 * LANES), BF16), pltpu.VMEM((VT_ROWS, n_keys), BF16)]
        + 2 * [pltpu.VMEM((n_sub, n_keys, tq), F32), pltpu.VMEM((n_sub, 8, tq), F32)],
        compiler_params=_params("parallel", "parallel", "arbitrary"),
        name="mla_attn" if latent else "ctx_mla_attn",
    )(*args)


def _diff_kernel(*refs, ctx, seq, latent, lambda_init, last):
    if latent:
        (q_ref, kc_ref, vc_ref, kl_ref, vl_ref, cosq_ref, sinq_ref, cosk_ref, sink_ref,
         lam_ref, sub_ref, o_ref, k_scr, v_scr, s0, m0, s1, m1) = refs
    else:
        q_ref, kc_ref, vc_ref, lam_ref, sub_ref, o_ref, k_scr, v_scr, s0, m0, s1, m1 = refs
    step = pl.program_id(2)

    @pl.when(step == 0)
    def _():
        k_scr[0:ctx, :] = kc_ref[...]
        _attn_fill_vt(v_scr, 0, vc_ref[...].astype(F32))
        if latent:
            k_scr[ctx:ctx + seq, :] = _rope(kl_ref[...].astype(F32), cosk_ref[...], sink_ref[...]).astype(BF16)
            _attn_fill_vt(v_scr, ctx, vl_ref[...].astype(F32))

    def body(park, parked):
        qs = None
        if park is not None:
            q = q_ref[...].astype(F32)
            if latent:
                q = _rope(q, cosq_ref[...], sinq_ref[...])
            q = q * (DIFF_SCALE * LOG2E)
            lane = lax.broadcasted_iota(jnp.int32, q.shape, 1)
            qs = [jnp.where(lane < DIFF_QK_DIM, q, 0.0).astype(BF16),
                  jnp.where(lane >= DIFF_QK_DIM, q, 0.0).astype(BF16)]
        acc = _attn_pipeline_step(qs, k_scr, v_scr, park, parked)
        if parked is not None:
            a1, a2 = acc
            dl = lam_ref[...]
            lam = (jnp.exp(jnp.sum(dl[0:1] * dl[1:2], axis=-1, keepdims=True))
                   - jnp.exp(jnp.sum(dl[2:3] * dl[3:4], axis=-1, keepdims=True)) + lambda_init)
            o_t = (a1[:HEAD_DIM] * (1.0 / a1[HEAD_DIM:HEAD_DIM + 1])
                   - a2[:HEAD_DIM] * (lam / a2[HEAD_DIM:HEAD_DIM + 1]))
            o_ref[...] = (_rms(o_t.T, sub_ref[...]) * (1.0 - lambda_init)).astype(BF16)

    _attn_pipeline(step, last, ((s0, m0), (s1, m1)), body)


def _diff_attn(pq, pc, px, cos, sin, dlam, subln, cols, *, batch, n_q, ctx, seq, latent, lambda_init):
    tq = _pick(n_q, (512, 256, 128))
    nq = n_q // tq
    qb, kb, vb = cols.dq // HEAD_DIM, cols.dk // HEAD_DIM, cols.dv // HEAD_DIM
    n_keys = ctx + (seq if latent else 0)
    q_tile = lambda i: jnp.minimum(i, nq - 1)
    o_tile = lambda i: jnp.maximum(i - 1, 0)
    in_specs = [
        pl.BlockSpec((tq, HEAD_DIM), lambda b, h, i: (b * nq + q_tile(i), qb + h)),
        pl.BlockSpec((ctx, HEAD_DIM), lambda b, h, i: (b, kb + h)),
        pl.BlockSpec((ctx, HEAD_DIM), lambda b, h, i: (b, vb + h)),
    ]
    args = [pq, pc, pc]
    if latent:
        in_specs += [
            pl.BlockSpec((seq, HEAD_DIM), lambda b, h, i: (b, kb + h)),
            pl.BlockSpec((seq, HEAD_DIM), lambda b, h, i: (b, vb + h)),
            pl.BlockSpec((tq, LANES), lambda b, h, i: (q_tile(i), 0)),
            pl.BlockSpec((tq, LANES), lambda b, h, i: (q_tile(i), 0)),
            pl.BlockSpec((seq, LANES), lambda b, h, i: (0, 0)),
            pl.BlockSpec((seq, LANES), lambda b, h, i: (0, 0)),
        ]
        args += [px, px, cos, sin, cos, sin]
    in_specs += [
        pl.BlockSpec((4, DIFF_QK_DIM), lambda b, h, i: (0, 0)),
        pl.BlockSpec((1, HEAD_DIM), lambda b, h, i: (0, 0)),
    ]
    args += [dlam, subln]
    return pl.pallas_call(
        functools.partial(_diff_kernel, ctx=ctx, seq=seq, latent=latent, lambda_init=lambda_init, last=nq),
        grid=(batch, N_HEADS_DIFF, nq + 1),
        in_specs=in_specs,
        out_specs=pl.BlockSpec((tq, HEAD_DIM), lambda b, h, i: (b * nq + o_tile(i), h)),
        out_shape=jax.ShapeDtypeStruct((batch * n_q, DIFF_W), BF16),
        scratch_shapes=[pltpu.VMEM((n_keys, HEAD_DIM), BF16), pltpu.VMEM((VT_ROWS, n_keys), BF16)]
        + 2 * [pltpu.VMEM((2, n_keys, tq), F32), pltpu.VMEM((2, 8, tq), F32)],
        compiler_params=_params("parallel", "parallel", "arbitrary"),
        name="diff_attn" if latent else "ctx_diff_attn",
    )(*args)


def _merge_kernel(oa_ref, ob_ref, od_ref, ga_ref, gb_ref, gd_ref, wa_ref, wb_ref, wd_ref, y_ref):
    y = (_sigmoid(ga_ref[...].astype(F32)) * _dot(oa_ref[...], wa_ref[...])
         + _sigmoid(gb_ref[...].astype(F32)) * _dot(ob_ref[...], wb_ref[...])
         + _sigmoid(gd_ref[...].astype(F32)) * _dot(od_ref[...], wd_ref[...]))
    y_ref[...] = y.astype(BF16)


def _merge(o_na, o_mla, o_diff, p, wb_na, wb_mla, wb_diff, cols, *, tm):
    n = o_na.shape[0]
    d = wb_na.shape[1]
    tn = _pick(d, (512, 256, 128))
    g0 = cols.gates // tn
    gs = d // tn
    return pl.pallas_call(
        _merge_kernel,
        grid=(n // tm, d // tn),
        in_specs=[
            pl.BlockSpec((tm, NA_W), lambda i, j: (i, 0)),
            pl.BlockSpec((tm, MLA_W), lambda i, j: (i, 0)),
            pl.BlockSpec((tm, DIFF_W), lambda i, j: (i, 0)),
            pl.BlockSpec((tm, tn), lambda i, j: (i, g0 + j)),
            pl.BlockSpec((tm, tn), lambda i, j: (i, g0 + gs + j)),
            pl.BlockSpec((tm, tn), lambda i, j: (i, g0 + 2 * gs + j)),
            pl.BlockSpec((NA_W, tn), lambda i, j: (0, j)),
            pl.BlockSpec((MLA_W, tn), lambda i, j: (0, j)),
            pl.BlockSpec((DIFF_W, tn), lambda i, j: (0, j)),
        ],
        out_specs=pl.BlockSpec((tm, tn), lambda i, j: (i, j)),
        out_shape=jax.ShapeDtypeStruct((n, d), BF16),
        compiler_params=_params("parallel", "parallel"),
        name="merge",
    )(o_na, o_mla, o_diff, p, p, p, wb_na, wb_mla, wb_diff)


def _outproj_kernel(x_ref, y_ref, mod_ref, w_ref, o_ref):
    o_ref[...] = x_ref[...] + mod_ref[...] * _dot(y_ref[...], w_ref[...].astype(BF16))


def _outproj(x, y, gate, w_out, *, l, rows_per_group, tm):
    n, d = x.shape
    tn = _pick(d, (512, 256, 128))
    tiles_per_group = rows_per_group // tm
    return pl.pallas_call(
        _outproj_kernel,
        grid=(n // tm, d // tn),
        in_specs=[
            pl.BlockSpec((tm, tn), lambda i, j: (i, j)),
            pl.BlockSpec((tm, d), lambda i, j: (i, 0)),
            pl.BlockSpec((None, 1, tn), lambda i, j: (i // tiles_per_group, 0, j)),
            pl.BlockSpec((None, d, tn), lambda i, j: (l, 0, j)),
        ],
        out_specs=pl.BlockSpec((tm, tn), lambda i, j: (i, j)),
        out_shape=jax.ShapeDtypeStruct((n, d), F32),
        compiler_params=_params("parallel", "parallel"),
        name="outproj",
    )(x, y, gate, w_out)


def _rope_tables(seq):
    quarter = MLA_ROPE_DIM // 4
    freqs = ROPE_THETA ** (-jnp.arange(quarter, dtype=F32) / quarter)
    t = jnp.arange(seq)
    rows, cols = t // GRID_W, t % GRID_W

    def unit(pos):
        ang = pos.astype(F32)[:, None] * freqs
        c, s = jnp.cos(ang), jnp.sin(ang)
        return jnp.concatenate([c, c], axis=1), jnp.concatenate([-s, s], axis=1)

    cr, sr = unit(rows)
    cc, sc = unit(cols)
    cos = jnp.concatenate([cr, cc], axis=1)
    sin = jnp.concatenate([sr, sc], axis=1)
    reps = LANES // MLA_ROPE_DIM
    return jnp.tile(cos, (1, reps)), jnp.tile(sin, (1, reps))


def _pack_heads(w, n_heads, width):
    k = w.shape[0]
    per = w.shape[1] // n_heads
    w = jnp.moveaxis(w.reshape(k, n_heads, per), 1, 0)
    if per < width:
        w = jnp.concatenate([w, jnp.zeros((n_heads, k, width - per), w.dtype)], axis=2)
    return w.astype(BF16)


def kernel(x, c, ctx, c_ctx, w_ada, b_ada, norm_w, ffn_w_in, ffn_w_out, w_in, na_rpb, mla_q_norm, mla_kv_norm,
           mla_w_uq, mla_w_ukv, diff_lambda, diff_subln, w_branch, w_out, final_norm):
    batch, seq, d = x.shape
    n_ctx = ctx.shape[1]
    depth = w_ada.shape[0]
    cols = _Cols(d)
    tm_x = _pick(seq, (1024, 512, 256, 128))
    tm_wide = _pick(seq, (2048, 1024, 512, 256, 128))
    tm_c = batch * n_ctx

    xs = x.reshape(batch * seq, d)
    xc = ctx.reshape(batch * n_ctx, d)
    cvec = jnp.concatenate([c, c_ctx[None, :], jnp.zeros((8 - batch - 1, d), F32)], axis=0)
    mods = _ada(cvec, w_ada, b_ada).reshape(depth, 8, N_MOD, d)
    cos, sin = _rope_tables(seq)
    wp = _pack_w_in(w_in)

    for l in range(depth):
        last = l == depth - 1
        lambda_init = 0.8 - 0.6 * math.exp(-0.3 * l)
        m = mods[l, :batch]
        mc = mods[l, batch:batch + 1]
        nw = norm_w[l].reshape(3, 1, d)

        xs = _ffn(xs, m, nw[0], ffn_w_in, ffn_w_out, l=l, s=0, mod0=0, rows_per_group=seq, tm=tm_x)
        xc = _ffn(xc, mc, nw[0], ffn_w_in, ffn_w_out, l=l, s=0, mod0=0, rows_per_group=tm_c, tm=tm_c)

        px = _inproj(xs, m, nw[1], wp, l=l, rows_per_group=seq, tm=tm_wide)
        pc = _inproj(xc, mc, nw[1], wp, l=l, rows_per_group=tm_c, tm=tm_c)

        bias = _na_bias(na_rpb[l], seq // GRID_W)
        qn = mla_q_norm[l].reshape(1, MLA_Q_RANK)
        kvn = mla_kv_norm[l].reshape(1, MLA_KV_RANK)
        wuq = _pack_heads(mla_w_uq[l], N_HEADS_MLA, 2 * LANES)
        wukv = _pack_heads(mla_w_ukv[l], N_HEADS_MLA, 2 * LANES)
        subln = diff_subln[l].reshape(1, HEAD_DIM)
        wb = w_branch[l].astype(BF16)
        wb_na, wb_mla, wb_diff = wb[:NA_W], wb[NA_W:NA_W + MLA_W], wb[NA_W + MLA_W:]

        o_na = _na_attn(px, pc, bias, cols, batch=batch, seq=seq, ctx=n_ctx)
        q_mla = _mla_q(px, cos, sin, qn, wuq, cols, n_q=seq, latent=True)
        o_mla = _mla_attn(q_mla, pc, px, cos, sin, kvn, wukv, cols,
                          batch=batch, n_q=seq, ctx=n_ctx, seq=seq, latent=True)
        o_diff = _diff_attn(px, pc, px, cos, sin, diff_lambda[l], subln, cols,
                            batch=batch, n_q=seq, ctx=n_ctx, seq=seq, latent=True, lambda_init=lambda_init)
        y = _merge(o_na, o_mla, o_diff, px, wb_na, wb_mla, wb_diff, cols, tm=tm_x)
        xs = _outproj(xs, y, m[:, 5:6], w_out, l=l, rows_per_group=seq, tm=tm_wide)
        xs = _ffn(xs, m, nw[2], ffn_w_in, ffn_w_out, l=l, s=1, mod0=6, rows_per_group=seq, tm=tm_x)

        if not last:
            oc_na = _ctx_na_attn(pc, cols, batch=batch, ctx=n_ctx)
            qc_mla = _mla_q(pc, cos, sin, qn, wuq, cols, n_q=n_ctx, latent=False)
            oc_mla = _mla_attn(qc_mla, pc, px, cos, sin, kvn, wukv, cols,
                               batch=batch, n_q=n_ctx, ctx=n_ctx, seq=seq, latent=False)
            oc_diff = _diff_attn(pc, pc, px, cos, sin, diff_lambda[l], subln, cols,
                                 batch=batch, n_q=n_ctx, ctx=n_ctx, seq=seq, latent=False, lambda_init=lambda_init)
            yc = _merge(oc_na, oc_mla, oc_diff, pc, wb_na, wb_mla, wb_diff, cols, tm=tm_c)
            xc = _outproj(xc, yc, mc[:, 5:6], w_out, l=l, rows_per_group=tm_c, tm=tm_c)
            xc = _ffn(xc, mc, nw[2], ffn_w_in, ffn_w_out, l=l, s=1, mod0=6, rows_per_group=tm_c, tm=tm_c)

    return _final_norm(xs, final_norm.reshape(1, d), tm=tm_x).reshape(batch, seq, d)
```

```python
import functools
import math

import jax
import jax.numpy as jnp
from jax import lax
from jax.experimental import pallas as pl
from jax.experimental.pallas import tpu as pltpu

GRID_W = 64
HEAD_DIM = 128
N_HEADS_NA = 6
N_HEADS_MLA = 5
N_HEADS_DIFF = 5
NA_KH = 8
NA_KW = 16
MLA_Q_RANK = 768
MLA_KV_RANK = 512
MLA_NOPE_DIM = 128
MLA_ROPE_DIM = 64
MLA_V_DIM = 128
DIFF_QK_DIM = HEAD_DIM // 2
N_MOD = 9
ROPE_THETA = 10000.0
NORM_EPS = 1e-6
NEG_INF = -1e30
NA_W = N_HEADS_NA * HEAD_DIM
MLA_W = N_HEADS_MLA * MLA_V_DIM
DIFF_W = N_HEADS_DIFF * HEAD_DIM
NA_SCALE = HEAD_DIM ** -0.5
MLA_SCALE = (MLA_NOPE_DIM + MLA_ROPE_DIM) ** -0.5
DIFF_SCALE = DIFF_QK_DIM ** -0.5

LOG2E = 1.4426950408889634
LANES = 128
VMEM_LIMIT = 60 * 1024 * 1024
BF16 = jnp.bfloat16
F32 = jnp.float32


def _params(*sem):
    return pltpu.CompilerParams(dimension_semantics=sem, vmem_limit_bytes=VMEM_LIMIT)


def _pick(n, prefs):
    for p in prefs:
        if n % p == 0:
            return p
    return n


def _dot(a, b):
    return jnp.dot(a, b, preferred_element_type=F32)


def _dot_nt(a, b):
    return lax.dot_general(a, b, (((1,), (1,)), ((), ())), preferred_element_type=F32)


def _rms(x, g):
    return x * lax.rsqrt(jnp.mean(x * x, axis=-1, keepdims=True) + NORM_EPS) * g


def _silu(x):
    return x * (1.0 / (1.0 + jnp.exp(-x)))


def _sigmoid(x):
    return 1.0 / (1.0 + jnp.exp(-x))


def _swap16(x):
    lane = lax.broadcasted_iota(jnp.int32, x.shape, x.ndim - 1)
    down = pltpu.roll(x, 16, x.ndim - 1)
    up = pltpu.roll(x, LANES - 16, x.ndim - 1)
    return jnp.where((lane & 16) != 0, down, up)


def _rope(x, cos, sin):
    return x * cos + _swap16(x) * sin


NORM_ROWS = 16


def _modulated_norm(h_ref, x_ref, nw, shift, scale):
    gain = nw * (1.0 + scale)

    def group(i, carry):
        r0 = pl.multiple_of(i * NORM_ROWS, NORM_ROWS)
        x = x_ref[pl.ds(r0, NORM_ROWS), :]
        rstd = lax.rsqrt(jnp.mean(x * x, axis=-1, keepdims=True) + NORM_EPS)
        h_ref[pl.ds(r0, NORM_ROWS), :] = (x * rstd * gain + shift).astype(BF16)
        return carry

    lax.fori_loop(0, x_ref.shape[0] // NORM_ROWS, group, 0, unroll=8)


def _ada_kernel(c_ref, w_ref, b_ref, o_ref):
    a = _silu(c_ref[...]).astype(BF16)
    o_ref[...] = _dot(a, w_ref[...].astype(BF16)) + b_ref[...]


def _ada(cvec, w_ada, b_ada):
    depth, d, n = w_ada.shape
    tn = _pick(n, (1024, 512, 256, 128))
    return pl.pallas_call(
        _ada_kernel,
        grid=(depth, n // tn),
        in_specs=[
            pl.BlockSpec((8, d), lambda l, j: (0, 0)),
            pl.BlockSpec((None, d, tn), lambda l, j: (l, 0, j)),
            pl.BlockSpec((None, 1, tn), lambda l, j: (l, 0, j)),
        ],
        out_specs=pl.BlockSpec((None, 8, tn), lambda l, j: (l, 0, j)),
        out_shape=jax.ShapeDtypeStruct((depth, 8, n), F32),
        compiler_params=_params("parallel", "parallel"),
        name="ada",
    )(cvec, w_ada, b_ada.reshape(depth, 1, n))


def _ffn_in_kernel(x_ref, mod_ref, nw_ref, wa_ref, wb_ref, g_ref, h_ref, *, mod0):
    @pl.when(pl.program_id(1) == 0)
    def _():
        _modulated_norm(h_ref, x_ref, nw_ref[...], mod_ref[mod0:mod0 + 1, :], mod_ref[mod0 + 1:mod0 + 2, :])

    h = h_ref[...]
    a = _dot(h, wa_ref[...].astype(BF16))
    b = _dot(h, wb_ref[...].astype(BF16))
    g_ref[...] = (_silu(a) * b).astype(BF16)


def _ffn_out_kernel(x_ref, g_ref, mod_ref, w_ref, o_ref, *, mod0):
    gate = 0.5 * mod_ref[mod0 + 2:mod0 + 3, :]
    o_ref[...] = x_ref[...] + gate * _dot(g_ref[...], w_ref[...].astype(BF16))


def _ffn(x, mod, nw, w_in, w_out, *, l, s, mod0, rows_per_group, tm):
    n, d = x.shape
    f = w_out.shape[2]
    fc = _pick(f, (512, 256, 128))
    nf = f // fc
    tn = _pick(d, (256, 128))
    tiles_per_group = rows_per_group // tm
    g = pl.pallas_call(
        functools.partial(_ffn_in_kernel, mod0=mod0),
        grid=(n // tm, nf),
        in_specs=[
            pl.BlockSpec((tm, d), lambda i, j: (i, 0)),
            pl.BlockSpec((None, N_MOD, d), lambda i, j: (i // tiles_per_group, 0, 0)),
            pl.BlockSpec((1, d), lambda i, j: (0, 0)),
            pl.BlockSpec((None, None, d, fc), lambda i, j: (l, s, 0, j)),
            pl.BlockSpec((None, None, d, fc), lambda i, j: (l, s, 0, nf + j)),
        ],
        out_specs=pl.BlockSpec((tm, fc), lambda i, j: (i, j)),
        out_shape=jax.ShapeDtypeStruct((n, f), BF16),
        scratch_shapes=[pltpu.VMEM((tm, d), BF16)],
        compiler_params=_params("parallel", "arbitrary"),
        name="ffn_in",
    )(x, mod, nw, w_in, w_in)
    return pl.pallas_call(
        functools.partial(_ffn_out_kernel, mod0=mod0),
        grid=(n // tm, d // tn),
        in_specs=[
            pl.BlockSpec((tm, tn), lambda i, j: (i, j)),
            pl.BlockSpec((tm, f), lambda i, j: (i, 0)),
            pl.BlockSpec((None, N_MOD, tn), lambda i, j: (i // tiles_per_group, 0, j)),
            pl.BlockSpec((None, None, f, tn), lambda i, j: (l, s, 0, j)),
        ],
        out_specs=pl.BlockSpec((tm, tn), lambda i, j: (i, j)),
        out_shape=jax.ShapeDtypeStruct((n, d), F32),
        compiler_params=_params("parallel", "parallel"),
        name="ffn_out",
    )(x, g, mod, w_out)


def _ffn_fused_kernel(x_ref, mod_ref, nw_ref, wa_ref, wb_ref, wo_ref, o_ref, h_ref, *, mod0):
    j = pl.program_id(1)

    @pl.when(j == 0)
    def _():
        _modulated_norm(h_ref, x_ref, nw_ref[...], mod_ref[mod0:mod0 + 1, :], mod_ref[mod0 + 1:mod0 + 2, :])

    h = h_ref[...]
    a = _dot(h, wa_ref[...].astype(BF16))
    b = _dot(h, wb_ref[...].astype(BF16))
    contrib = _dot((_silu(a) * b).astype(BF16), wo_ref[...].astype(BF16))

    @pl.when(j == 0)
    def _():
        o_ref[...] = contrib

    @pl.when(j > 0)
    def _():
        o_ref[...] += contrib

    @pl.when(j == pl.num_programs(1) - 1)
    def _():
        o_ref[...] = x_ref[...] + (0.5 * mod_ref[mod0 + 2:mod0 + 3, :]) * o_ref[...]


def _ffn_fused(x, mod, nw, w_in, w_out, *, l, s, mod0):
    n, d = x.shape
    f = w_out.shape[2]
    fc = _pick(f, (256, 128))
    nf = f // fc
    return pl.pallas_call(
        functools.partial(_ffn_fused_kernel, mod0=mod0),
        grid=(1, nf),
        in_specs=[
            pl.BlockSpec((n, d), lambda i, j: (0, 0)),
            pl.BlockSpec((None, N_MOD, d), lambda i, j: (0, 0, 0)),
            pl.BlockSpec((1, d), lambda i, j: (0, 0)),
            pl.BlockSpec((None, None, d, fc), lambda i, j: (l, s, 0, j)),
            pl.BlockSpec((None, None, d, fc), lambda i, j: (l, s, 0, nf + j)),
            pl.BlockSpec((None, None, fc, d), lambda i, j: (l, s, j, 0)),
        ],
        out_specs=pl.BlockSpec((n, d), lambda i, j: (0, 0)),
        out_shape=jax.ShapeDtypeStruct((n, d), F32),
        scratch_shapes=[pltpu.VMEM((n, d), BF16)],
        compiler_params=_params("arbitrary", "arbitrary"),
        name="ffn_fused",
    )(x, mod, nw, w_in, w_in, w_out)


def _final_norm_kernel(x_ref, g_ref, o_ref):
    o_ref[...] = _rms(x_ref[...], g_ref[...])


def _final_norm(x, g, *, tm):
    n, d = x.shape
    return pl.pallas_call(
        _final_norm_kernel,
        grid=(n // tm,),
        in_specs=[pl.BlockSpec((tm, d), lambda i: (i, 0)), pl.BlockSpec((1, d), lambda i: (0, 0))],
        out_specs=pl.BlockSpec((tm, d), lambda i: (i, 0)),
        out_shape=jax.ShapeDtypeStruct((n, d), F32),
        compiler_params=_params("parallel"),
        name="final_norm",
    )(x, g)


def _inproj_kernel(x_ref, mod_ref, nw_ref, w_ref, o_ref, h_ref):
    @pl.when(pl.program_id(1) == 0)
    def _():
        _modulated_norm(h_ref, x_ref, nw_ref[...], mod_ref[3:4, :], mod_ref[4:5, :])

    o_ref[...] = _dot_nt(h_ref[...], w_ref[...]).astype(BF16)


def _inproj(x, mod, nw, w, *, l, rows_per_group, tm):
    n, d = x.shape
    nw_cols = w.shape[1]
    tn = _pick(nw_cols, (512, 256, 128))
    tiles_per_group = rows_per_group // tm
    return pl.pallas_call(
        _inproj_kernel,
        grid=(n // tm, nw_cols // tn),
        in_specs=[
            pl.BlockSpec((tm, d), lambda i, j: (i, 0)),
            pl.BlockSpec((None, N_MOD, d), lambda i, j: (i // tiles_per_group, 0, 0)),
            pl.BlockSpec((1, d), lambda i, j: (0, 0)),
            pl.BlockSpec((None, tn, d), lambda i, j: (l, j, 0)),
        ],
        out_specs=pl.BlockSpec((tm, tn), lambda i, j: (i, j)),
        out_shape=jax.ShapeDtypeStruct((n, nw_cols), BF16),
        scratch_shapes=[pltpu.VMEM((tm, d), BF16)],
        compiler_params=_params("parallel", "arbitrary"),
        name="inproj",
    )(x, mod, nw, w)


class _Cols:
    def __init__(self, d):
        self.na_q = 0
        self.na_k = NA_W
        self.na_v = 2 * NA_W
        self.cq = 3 * NA_W
        self.ckv = self.cq + MLA_Q_RANK
        self.gates = self.ckv + MLA_KV_RANK
        self.dq = self.gates + 3 * d
        self.dk = self.dq + DIFF_W
        self.dv = self.dk + DIFF_W
        self.kpe = self.dv + DIFF_W
        self.total = self.kpe + LANES


def _pack_w_in_kernel(w_ref, o_ref):
    w = w_ref[...]
    o = 3 * NA_W + MLA_Q_RANK + MLA_KV_RANK
    o2 = o + MLA_ROPE_DIM
    o3 = o2 + 3 * DIFF_W
    pad = jnp.zeros((LANES - MLA_ROPE_DIM, w.shape[1]), w.dtype)
    o_ref[...] = jnp.concatenate([w[:o], w[o3:], w[o2:o3], w[o:o2], pad], axis=0).astype(BF16)


def _pack_w_in(w_in):
    depth, d, n = w_in.shape
    tc = _pick(d, (256, 128))
    n_out = n + LANES - MLA_ROPE_DIM
    return pl.pallas_call(
        _pack_w_in_kernel,
        grid=(depth, d // tc),
        in_specs=[pl.BlockSpec((None, n, tc), lambda l, i: (l, 0, i))],
        out_specs=pl.BlockSpec((None, n_out, tc), lambda l, i: (l, 0, i)),
        out_shape=jax.ShapeDtypeStruct((depth, n_out, d), BF16),
        compiler_params=_params("parallel", "parallel"),
        name="pack_w_in",
    )(jnp.swapaxes(w_in, 1, 2))


def _softmax_pv(parts):
    mx = parts[0][0].max(axis=-1, keepdims=True)
    for s, _ in parts[1:]:
        mx = jnp.maximum(mx, s.max(axis=-1, keepdims=True))
    acc = None
    den = None
    for s, v in parts:
        e = jnp.exp(s - mx)
        den_i = e.sum(axis=-1, keepdims=True)
        o_i = _dot(e.astype(BF16), v)
        acc = o_i if acc is None else acc + o_i
        den = den_i if den is None else den + den_i
    return acc * (1.0 / den)


NA_ROWS = 4
NA_UNION = NA_ROWS + NA_KH


def _na_union_start(r, n_rows):
    if isinstance(r, int):
        return min(min(max(r - NA_KH // 2, 0), n_rows - NA_KH), n_rows - NA_UNION)
    return jnp.minimum(jnp.clip(r - NA_KH // 2, 0, n_rows - NA_KH), n_rows - NA_UNION)


def _na_kernel(q_ref, k_ref, v_ref, kc_ref, vc_ref, bias_ref, o_ref, *, n_rows):
    u0 = _na_union_start(pl.program_id(1) * NA_ROWS, n_rows)
    k0 = pl.multiple_of(u0 * GRID_W, GRID_W)
    n_loc = NA_UNION * GRID_W
    c = NA_SCALE * LOG2E
    cs = [slice(h * HEAD_DIM, (h + 1) * HEAD_DIM) for h in range(N_HEADS_NA)]
    def scores(h):
        return (_dot_nt(q_ref[:, cs[h]], k_ref[pl.ds(k0, n_loc), cs[h]]) * c + bias_ref[h],
                _dot_nt(q_ref[:, cs[h]], kc_ref[:, cs[h]]) * c)

    nxt = scores(0)
    for h in range(N_HEADS_NA):
        s_loc, s_ctx = nxt
        if h + 1 < N_HEADS_NA:
            nxt = scores(h + 1)
        m = jnp.maximum(s_loc.max(axis=-1, keepdims=True), s_ctx.max(axis=-1, keepdims=True))
        e_loc = jnp.exp2(s_loc - m)
        e_ctx = jnp.exp2(s_ctx - m)
        den = e_loc.sum(axis=-1, keepdims=True) + e_ctx.sum(axis=-1, keepdims=True)
        o = _dot(e_loc.astype(BF16), v_ref[pl.ds(k0, n_loc), cs[h]]) + _dot(e_ctx.astype(BF16), vc_ref[:, cs[h]])
        o_ref[:, cs[h]] = (o * (1.0 / den)).astype(BF16)


def _na_bias(rpb, n_rows):
    assert n_rows >= NA_UNION and n_rows % NA_ROWS == 0
    h, n_ro, _ = rpb.shape
    w = GRID_W
    col = jnp.arange(w)
    c0 = jnp.clip(col - NA_KW // 2, 0, w - NA_KW)
    col_ok = (col[None, :] >= c0[:, None]) & (col[None, :] < c0[:, None] + NA_KW)
    edge = w - NA_KW
    rpb = rpb.astype(F32)
    text = jnp.concatenate([jnp.broadcast_to(rpb[..., :1], (h, n_ro, edge)), rpb,
                            jnp.broadcast_to(rpb[..., -1:], (h, n_ro, edge)),
                            jnp.zeros((h, n_ro, 1), F32)], axis=-1)
    skew = jnp.tile(text, (1, 1, w))[..., :w * (2 * w - 1)].reshape(h, n_ro, w, 2 * w - 1)
    t = jnp.where(col_ok[None, None], skew[..., w - 1:], NEG_INF) * LOG2E
    masked = jnp.full((h, w, w), NEG_INF * LOG2E, F32)
    kinds = []
    for r in (0, NA_ROWS, n_rows - NA_ROWS):
        u0 = _na_union_start(r, n_rows)
        q_rows = []
        for qr in range(r, r + NA_ROWS):
            r0 = min(max(qr - NA_KH // 2, 0), n_rows - NA_KH)
            tiles = [t[:, kr - qr + NA_KH - 1] if r0 <= kr < r0 + NA_KH else masked
                     for kr in range(u0, u0 + NA_UNION)]
            q_rows.append(jnp.concatenate(tiles, axis=-1))
        kinds.append(jnp.concatenate(q_rows, axis=1))
    return jnp.stack(kinds, axis=0)


def _na_attn(px, pc, bias, cols, *, batch, seq, ctx):
    n_rows = seq // GRID_W
    steps = n_rows // NA_ROWS
    tq = NA_ROWS * GRID_W
    qb, kb, vb = cols.na_q // NA_W, cols.na_k // NA_W, cols.na_v // NA_W
    kind = lambda i: jnp.where(i == 0, 0, jnp.where(i == steps - 1, 2, 1))
    return pl.pallas_call(
        functools.partial(_na_kernel, n_rows=n_rows),
        grid=(batch, steps),
        in_specs=[
            pl.BlockSpec((tq, NA_W), lambda b, i: (b * steps + i, qb)),
            pl.BlockSpec((seq, NA_W), lambda b, i: (b, kb)),
            pl.BlockSpec((seq, NA_W), lambda b, i: (b, vb)),
            pl.BlockSpec((ctx, NA_W), lambda b, i: (b, kb)),
            pl.BlockSpec((ctx, NA_W), lambda b, i: (b, vb)),
            pl.BlockSpec((None,) + bias.shape[1:], lambda b, i: (kind(i), 0, 0, 0)),
        ],
        out_specs=pl.BlockSpec((tq, NA_W), lambda b, i: (b * steps + i, 0)),
        out_shape=jax.ShapeDtypeStruct((batch * seq, NA_W), BF16),
        compiler_params=_params("parallel", "arbitrary"),
        name="na_attn",
    )(px, px, px, pc, pc, bias)


def _ctx_na_kernel(q_ref, k_ref, v_ref, o_ref):
    for h in range(N_HEADS_NA):
        cs = slice(h * HEAD_DIM, (h + 1) * HEAD_DIM)
        s = _dot_nt(q_ref[:, cs], k_ref[:, cs]) * NA_SCALE
        o_ref[:, cs] = _softmax_pv([(s, v_ref[:, cs])]).astype(BF16)


def _ctx_na_attn(pc, cols, *, batch, ctx):
    qb, kb, vb = cols.na_q // NA_W, cols.na_k // NA_W, cols.na_v // NA_W
    return pl.pallas_call(
        _ctx_na_kernel,
        grid=(batch,),
        in_specs=[
            pl.BlockSpec((ctx, NA_W), lambda b: (b, qb)),
            pl.BlockSpec((ctx, NA_W), lambda b: (b, kb)),
            pl.BlockSpec((ctx, NA_W), lambda b: (b, vb)),
        ],
        out_specs=pl.BlockSpec((ctx, NA_W), lambda b: (b, 0)),
        out_shape=jax.ShapeDtypeStruct((batch * ctx, NA_W), BF16),
        compiler_params=_params("parallel"),
        name="ctx_na_attn",
    )(pc, pc, pc)


VT_ROWS = 144
ATTN_CHUNK = 256


def _attn_pipeline_step(qs, k_scr, vt_scr, park, parked):
    n_sets = park[0].shape[0] if park is not None else parked[0].shape[0]
    n_keys = k_scr.shape[0]
    sizes = ([n_keys % ATTN_CHUNK] if n_keys % ATTN_CHUNK else []) + [ATTN_CHUNK] * (n_keys // ATTN_CHUNK)
    run = [None] * n_sets
    acc = [None] * n_sets
    start = 0
    for ch in sizes:
        ks = slice(start, start + ch)
        start += ch
        for t in range(n_sets):
            if park is not None:
                s = _dot_nt(k_scr[ks, :], qs[t])
                park[0][t, ks, :] = s
                m = s.reshape(ch // 8, 8, s.shape[1]).max(axis=0)
                run[t] = m if run[t] is None else jnp.maximum(run[t], m)
            if parked is not None:
                sp = parked[0][t, ks, :]
                e = jnp.exp2(sp.reshape(ch // 8, 8, sp.shape[1]) - parked[1][t][None]).reshape(sp.shape)
                pv = _dot(vt_scr[:, ks], e.astype(BF16))
                acc[t] = pv if acc[t] is None else acc[t] + pv
    if park is not None:
        for t in range(n_sets):
            park[1][t] = jnp.broadcast_to(run[t].max(axis=0, keepdims=True), run[t].shape)
    return acc


def _attn_fill_vt(vt_scr, col0, v):
    n = v.shape[0]
    vt_scr[0:HEAD_DIM, col0:col0 + n] = v.T.astype(BF16)
    vt_scr[HEAD_DIM:, col0:col0 + n] = jnp.ones((VT_ROWS - HEAD_DIM, n), BF16)


def _attn_pipeline(step, last, bufs, body):
    @pl.when(step == 0)
    def _():
        body(bufs[0], None)

    if last > 1:
        @pl.when((step > 0) & (step < last) & (step % 2 == 0))
        def _():
            body(bufs[0], bufs[1])

        @pl.when((step < last) & (step % 2 == 1))
        def _():
            body(bufs[1], bufs[0])

    @pl.when(step == last)
    def _():
        body(None, bufs[(last - 1) % 2])


def _mla_q_kernel(*refs, latent):
    if latent:
        cq_ref, cos_ref, sin_ref, qn_ref, wuq_ref, o_ref = refs
    else:
        cq_ref, qn_ref, wuq_ref, o_ref = refs
    n = _rms(cq_ref[...].astype(F32), qn_ref[...]).astype(BF16)
    for h in range(N_HEADS_MLA):
        q = _dot(n, wuq_ref[h])
        q_pe = q[:, MLA_NOPE_DIM:]
        if latent:
            q_pe = _rope(q_pe, cos_ref[...], sin_ref[...])
        q = jnp.concatenate([q[:, :MLA_NOPE_DIM], q_pe], axis=1) * (MLA_SCALE * LOG2E)
        o_ref[:, h * 2 * LANES:(h + 1) * 2 * LANES] = q.astype(BF16)


def _mla_q(pq, cos, sin, qn, wuq, cols, *, n_q, latent):
    n = pq.shape[0]
    tm = _pick(n_q, (512, 256, 128))
    tiles = n_q // tm
    cqb = cols.cq // MLA_Q_RANK
    in_specs = [pl.BlockSpec((tm, MLA_Q_RANK), lambda i: (i, cqb))]
    args = [pq]
    if latent:
        in_specs += [pl.BlockSpec((tm, LANES), lambda i: (i % tiles, 0))] * 2
        args += [cos, sin]
    in_specs += [pl.BlockSpec((1, MLA_Q_RANK), lambda i: (0, 0)),
                 pl.BlockSpec(wuq.shape, lambda i: (0, 0, 0))]
    args += [qn, wuq]
    return pl.pallas_call(
        functools.partial(_mla_q_kernel, latent=latent),
        grid=(n // tm,),
        in_specs=in_specs,
        out_specs=pl.BlockSpec((tm, N_HEADS_MLA * 2 * LANES), lambda i: (i, 0)),
        out_shape=jax.ShapeDtypeStruct((n, N_HEADS_MLA * 2 * LANES), BF16),
        compiler_params=_params("parallel"),
        name="mla_q" if latent else "ctx_mla_q",
    )(*args)


def _mla_kernel(*refs, ctx, seq, latent, last):
    if latent:
        (q_ref, ckvc_ref, kpec_ref, ckvl_ref, kpel_ref, cosk_ref, sink_ref,
         kvn_ref, wukv_ref, o_ref, k_scr, v_scr, s0, m0, s1, m1) = refs
    else:
        q_ref, ckvc_ref, kpec_ref, kvn_ref, wukv_ref, o_ref, k_scr, v_scr, s0, m0, s1, m1 = refs
    step = pl.program_id(2)

    @pl.when(step == 0)
    def _():
        w = wukv_ref[...]

        def fill(row0, n, ckv_ref, kpe):
            kv = _dot(_rms(ckv_ref[...].astype(F32), kvn_ref[...]).astype(BF16), w)
            k_scr[row0:row0 + n, 0:MLA_NOPE_DIM] = kv[:, :MLA_NOPE_DIM].astype(BF16)
            k_scr[row0:row0 + n, MLA_NOPE_DIM:] = kpe.astype(BF16)
            _attn_fill_vt(v_scr, row0, kv[:, MLA_NOPE_DIM:])

        fill(0, ctx, ckvc_ref, kpec_ref[...])
        if latent:
            fill(ctx, seq, ckvl_ref, _rope(kpel_ref[...].astype(F32), cosk_ref[...], sink_ref[...]))

    n_sub, _, tq = s0.shape

    def body(park, parked):
        qs = [q_ref[t * tq:(t + 1) * tq, :] for t in range(n_sub)] if park is not None else None
        acc = _attn_pipeline_step(qs, k_scr, v_scr, park, parked)
        if parked is not None:
            for t in range(n_sub):
                o_t = acc[t][:MLA_V_DIM] * (1.0 / acc[t][MLA_V_DIM:MLA_V_DIM + 1])
                o_ref[t * tq:(t + 1) * tq, :] = o_t.T.astype(BF16)

    _attn_pipeline(step, last, ((s0, m0), (s1, m1)), body)


def _mla_attn(q, pc, px, cos, sin, kvn, wukv, cols, *, batch, n_q, ctx, seq, latent):
    tq = _pick(n_q, (256, 128))
    n_sub = 2 if n_q % (2 * tq) == 0 else 1
    rows = n_sub * tq
    nq = n_q // rows
    ckvb, kpeb = cols.ckv // MLA_KV_RANK, cols.kpe // LANES
    n_keys = ctx + (seq if latent else 0)
    q_tile = lambda i: jnp.minimum(i, nq - 1)
    o_tile = lambda i: jnp.maximum(i - 1, 0)
    in_specs = [
        pl.BlockSpec((rows, 2 * LANES), lambda b, h, i: (b * nq + q_tile(i), h)),
        pl.BlockSpec((ctx, MLA_KV_RANK), lambda b, h, i: (b, ckvb)),
        pl.BlockSpec((ctx, LANES), lambda b, h, i: (b, kpeb)),
    ]
    args = [q, pc, pc]
    if latent:
        in_specs += [
            pl.BlockSpec((seq, MLA_KV_RANK), lambda b, h, i: (b, ckvb)),
            pl.BlockSpec((seq, LANES), lambda b, h, i: (b, kpeb)),
            pl.BlockSpec((seq, LANES), lambda b, h, i: (0, 0)),
            pl.BlockSpec((seq, LANES), lambda b, h, i: (0, 0)),
        ]
        args += [px, px, cos, sin]
    in_specs += [
        pl.BlockSpec((1, MLA_KV_RANK), lambda b, h, i: (0, 0)),
        pl.BlockSpec((None, MLA_KV_RANK, 2 * LANES), lambda b, h, i: (h, 0, 0)),
    ]
    args += [kvn, wukv]
    return pl.pallas_call(
        functools.partial(_mla_kernel, ctx=ctx, seq=seq, latent=latent, last=nq),
        grid=(batch, N_HEADS_MLA, nq + 1),
        in_specs=in_specs,
        out_specs=pl.BlockSpec((rows, MLA_V_DIM), lambda b, h, i: (b * nq + o_tile(i), h)),
        out_shape=jax.ShapeDtypeStruct((batch * n_q, MLA_W), BF16),
        scratch_shapes=[pltpu.VMEM((n_keys, 2 * LANES), BF16), pltpu.VMEM((VT_ROWS, n_keys), BF16)]
        + 2 * [pltpu.VMEM((n_sub, n_keys, tq), F32), pltpu.VMEM((n_sub, 8, tq), F32)],
        compiler_params=_params("parallel", "parallel", "arbitrary"),
        name="mla_attn" if latent else "ctx_mla_attn",
    )(*args)


def _diff_kernel(*refs, ctx, seq, latent, lambda_init, last):
    if latent:
        (q_ref, kc_ref, vc_ref, kl_ref, vl_ref, cosq_ref, sinq_ref, cosk_ref, sink_ref,
         lam_ref, sub_ref, o_ref, k_scr, v_scr, s0, m0, s1, m1) = refs
    else:
        q_ref, kc_ref, vc_ref, lam_ref, sub_ref, o_ref, k_scr, v_scr, s0, m0, s1, m1 = refs
    step = pl.program_id(2)

    @pl.when(step == 0)
    def _():
        k_scr[0:ctx, :] = kc_ref[...]
        _attn_fill_vt(v_scr, 0, vc_ref[...].astype(F32))
        if latent:
            k_scr[ctx:ctx + seq, :] = _rope(kl_ref[...].astype(F32), cosk_ref[...], sink_ref[...]).astype(BF16)
            _attn_fill_vt(v_scr, ctx, vl_ref[...].astype(F32))

    def body(park, parked):
        qs = None
        if park is not None:
            q = q_ref[...].astype(F32)
            if latent:
                q = _rope(q, cosq_ref[...], sinq_ref[...])
            q = q * (DIFF_SCALE * LOG2E)
            lane = lax.broadcasted_iota(jnp.int32, q.shape, 1)
            qs = [jnp.where(lane < DIFF_QK_DIM, q, 0.0).astype(BF16),
                  jnp.where(lane >= DIFF_QK_DIM, q, 0.0).astype(BF16)]
        acc = _attn_pipeline_step(qs, k_scr, v_scr, park, parked)
        if parked is not None:
            a1, a2 = acc
            dl = lam_ref[...]
            lam = (jnp.exp(jnp.sum(dl[0:1] * dl[1:2], axis=-1, keepdims=True))
                   - jnp.exp(jnp.sum(dl[2:3] * dl[3:4], axis=-1, keepdims=True)) + lambda_init)
            o_t = (a1[:HEAD_DIM] * (1.0 / a1[HEAD_DIM:HEAD_DIM + 1])
                   - a2[:HEAD_DIM] * (lam / a2[HEAD_DIM:HEAD_DIM + 1]))
            o_ref[...] = (_rms(o_t.T, sub_ref[...]) * (1.0 - lambda_init)).astype(BF16)

    _attn_pipeline(step, last, ((s0, m0), (s1, m1)), body)


def _diff_attn(pq, pc, px, cos, sin, dlam, subln, cols, *, batch, n_q, ctx, seq, latent, lambda_init):
    tq = _pick(n_q, (512, 256, 128))
    nq = n_q // tq
    qb, kb, vb = cols.dq // HEAD_DIM, cols.dk // HEAD_DIM, cols.dv // HEAD_DIM
    n_keys = ctx + (seq if latent else 0)
    q_tile = lambda i: jnp.minimum(i, nq - 1)
    o_tile = lambda i: jnp.maximum(i - 1, 0)
    in_specs = [
        pl.BlockSpec((tq, HEAD_DIM), lambda b, h, i: (b * nq + q_tile(i), qb + h)),
        pl.BlockSpec((ctx, HEAD_DIM), lambda b, h, i: (b, kb + h)),
        pl.BlockSpec((ctx, HEAD_DIM), lambda b, h, i: (b, vb + h)),
    ]
    args = [pq, pc, pc]
    if latent:
        in_specs += [
            pl.BlockSpec((seq, HEAD_DIM), lambda b, h, i: (b, kb + h)),
            pl.BlockSpec((seq, HEAD_DIM), lambda b, h, i: (b, vb + h)),
            pl.BlockSpec((tq, LANES), lambda b, h, i: (q_tile(i), 0)),
            pl.BlockSpec((tq, LANES), lambda b, h, i: (q_tile(i), 0)),
            pl.BlockSpec((seq, LANES), lambda b, h, i: (0, 0)),
            pl.BlockSpec((seq, LANES), lambda b, h, i: (0, 0)),
        ]
        args += [px, px, cos, sin, cos, sin]
    in_specs += [
        pl.BlockSpec((4, DIFF_QK_DIM), lambda b, h, i: (0, 0)),
        pl.BlockSpec((1, HEAD_DIM), lambda b, h, i: (0, 0)),
    ]
    args += [dlam, subln]
    return pl.pallas_call(
        functools.partial(_diff_kernel, ctx=ctx, seq=seq, latent=latent, lambda_init=lambda_init, last=nq),
        grid=(batch, N_HEADS_DIFF, nq + 1),
        in_specs=in_specs,
        out_specs=pl.BlockSpec((tq, HEAD_DIM), lambda b, h, i: (b * nq + o_tile(i), h)),
        out_shape=jax.ShapeDtypeStruct((batch * n_q, DIFF_W), BF16),
        scratch_shapes=[pltpu.VMEM((n_keys, HEAD_DIM), BF16), pltpu.VMEM((VT_ROWS, n_keys), BF16)]
        + 2 * [pltpu.VMEM((2, n_keys, tq), F32), pltpu.VMEM((2, 8, tq), F32)],
        compiler_params=_params("parallel", "parallel", "arbitrary"),
        name="diff_attn" if latent else "ctx_diff_attn",
    )(*args)


def _merge_kernel(oa_ref, ob_ref, od_ref, ga_ref, gb_ref, gd_ref, wa_ref, wb_ref, wd_ref, y_ref):
    y = (_sigmoid(ga_ref[...].astype(F32)) * _dot(oa_ref[...], wa_ref[...])
         + _sigmoid(gb_ref[...].astype(F32)) * _dot(ob_ref[...], wb_ref[...])
         + _sigmoid(gd_ref[...].astype(F32)) * _dot(od_ref[...], wd_ref[...]))
    y_ref[...] = y.astype(BF16)


def _merge(o_na, o_mla, o_diff, p, wb_na, wb_mla, wb_diff, cols, *, tm):
    n = o_na.shape[0]
    d = wb_na.shape[1]
    tn = _pick(d, (512, 256, 128))
    g0 = cols.gates // tn
    gs = d // tn
    return pl.pallas_call(
        _merge_kernel,
        grid=(n // tm, d // tn),
        in_specs=[
            pl.BlockSpec((tm, NA_W), lambda i, j: (i, 0)),
            pl.BlockSpec((tm, MLA_W), lambda i, j: (i, 0)),
            pl.BlockSpec((tm, DIFF_W), lambda i, j: (i, 0)),
            pl.BlockSpec((tm, tn), lambda i, j: (i, g0 + j)),
            pl.BlockSpec((tm, tn), lambda i, j: (i, g0 + gs + j)),
            pl.BlockSpec((tm, tn), lambda i, j: (i, g0 + 2 * gs + j)),
            pl.BlockSpec((NA_W, tn), lambda i, j: (0, j)),
            pl.BlockSpec((MLA_W, tn), lambda i, j: (0, j)),
            pl.BlockSpec((DIFF_W, tn), lambda i, j: (0, j)),
        ],
        out_specs=pl.BlockSpec((tm, tn), lambda i, j: (i, j)),
        out_shape=jax.ShapeDtypeStruct((n, d), BF16),
        compiler_params=_params("parallel", "parallel"),
        name="merge",
    )(o_na, o_mla, o_diff, p, p, p, wb_na, wb_mla, wb_diff)


def _outproj_kernel(x_ref, y_ref, mod_ref, w_ref, o_ref):
    o_ref[...] = x_ref[...] + mod_ref[...] * _dot(y_ref[...], w_ref[...].astype(BF16))


def _outproj(x, y, gate, w_out, *, l, rows_per_group, tm):
    n, d = x.shape
    tn = _pick(d, (512, 256, 128))
    tiles_per_group = rows_per_group // tm
    return pl.pallas_call(
        _outproj_kernel,
        grid=(n // tm, d // tn),
        in_specs=[
            pl.BlockSpec((tm, tn), lambda i, j: (i, j)),
            pl.BlockSpec((tm, d), lambda i, j: (i, 0)),
            pl.BlockSpec((None, 1, tn), lambda i, j: (i // tiles_per_group, 0, j)),
            pl.BlockSpec((None, d, tn), lambda i, j: (l, 0, j)),
        ],
        out_specs=pl.BlockSpec((tm, tn), lambda i, j: (i, j)),
        out_shape=jax.ShapeDtypeStruct((n, d), F32),
        compiler_params=_params("parallel", "parallel"),
        name="outproj",
    )(x, y, gate, w_out)


def _rope_tables(seq):
    quarter = MLA_ROPE_DIM // 4
    freqs = ROPE_THETA ** (-jnp.arange(quarter, dtype=F32) / quarter)
    t = jnp.arange(seq)
    rows, cols = t // GRID_W, t % GRID_W

    def unit(pos):
        ang = pos.astype(F32)[:, None] * freqs
        c, s = jnp.cos(ang), jnp.sin(ang)
        return jnp.concatenate([c, c], axis=1), jnp.concatenate([-s, s], axis=1)

    cr, sr = unit(rows)
    cc, sc = unit(cols)
    cos = jnp.concatenate([cr, cc], axis=1)
    sin = jnp.concatenate([sr, sc], axis=1)
    reps = LANES // MLA_ROPE_DIM
    return jnp.tile(cos, (1, reps)), jnp.tile(sin, (1, reps))


def _pack_heads(w, n_heads, width):
    k = w.shape[0]
    per = w.shape[1] // n_heads
    w = jnp.moveaxis(w.reshape(k, n_heads, per), 1, 0)
    if per < width:
        w = jnp.concatenate([w, jnp.zeros((n_heads, k, width - per), w.dtype)], axis=2)
    return w.astype(BF16)


def kernel(x, c, ctx, c_ctx, w_ada, b_ada, norm_w, ffn_w_in, ffn_w_out, w_in, na_rpb, mla_q_norm, mla_kv_norm,
           mla_w_uq, mla_w_ukv, diff_lambda, diff_subln, w_branch, w_out, final_norm):
    batch, seq, d = x.shape
    n_ctx = ctx.shape[1]
    depth = w_ada.shape[0]
    cols = _Cols(d)
    tm_x = _pick(seq, (1024, 512, 256, 128))
    tm_wide = _pick(seq, (2048, 1024, 512, 256, 128))
    tm_c = batch * n_ctx

    xs = x.reshape(batch * seq, d)
    xc = ctx.reshape(batch * n_ctx, d)
    cvec = jnp.concatenate([c, c_ctx[None, :], jnp.zeros((8 - batch - 1, d), F32)], axis=0)
    mods = _ada(cvec, w_ada, b_ada).reshape(depth, 8, N_MOD, d)
    cos, sin = _rope_tables(seq)
    wp = _pack_w_in(w_in)

    for l in range(depth):
        last = l == depth - 1
        lambda_init = 0.8 - 0.6 * math.exp(-0.3 * l)
        m = mods[l, :batch]
        mc = mods[l, batch:batch + 1]
        nw = norm_w[l].reshape(3, 1, d)

        xs = _ffn(xs, m, nw[0], ffn_w_in, ffn_w_out, l=l, s=0, mod0=0, rows_per_group=seq, tm=tm_x)
        xc = _ffn_fused(xc, mc, nw[0], ffn_w_in, ffn_w_out, l=l, s=0, mod0=0)

        px = _inproj(xs, m, nw[1], wp, l=l, rows_per_group=seq, tm=tm_wide)
        pc = _inproj(xc, mc, nw[1], wp, l=l, rows_per_group=tm_c, tm=tm_c)

        bias = _na_bias(na_rpb[l], seq // GRID_W)
        qn = mla_q_norm[l].reshape(1, MLA_Q_RANK)
        kvn = mla_kv_norm[l].reshape(1, MLA_KV_RANK)
        wuq = _pack_heads(mla_w_uq[l], N_HEADS_MLA, 2 * LANES)
        wukv = _pack_heads(mla_w_ukv[l], N_HEADS_MLA, 2 * LANES)
        subln = diff_subln[l].reshape(1, HEAD_DIM)
        wb = w_branch[l].astype(BF16)
        wb_na, wb_mla, wb_diff = wb[:NA_W], wb[NA_W:NA_W + MLA_W], wb[NA_W + MLA_W:]

        o_na = _na_attn(px, pc, bias, cols, batch=batch, seq=seq, ctx=n_ctx)
        q_mla = _mla_q(px, cos, sin, qn, wuq, cols, n_q=seq, latent=True)
        o_mla = _mla_attn(q_mla, pc, px, cos, sin, kvn, wukv, cols,
                          batch=batch, n_q=seq, ctx=n_ctx, seq=seq, latent=True)
        o_diff = _diff_attn(px, pc, px, cos, sin, diff_lambda[l], subln, cols,
                            batch=batch, n_q=seq, ctx=n_ctx, seq=seq, latent=True, lambda_init=lambda_init)
        y = _merge(o_na, o_mla, o_diff, px, wb_na, wb_mla, wb_diff, cols, tm=tm_x)
        xs = _outproj(xs, y, m[:, 5:6], w_out, l=l, rows_per_group=seq, tm=tm_wide)
        xs = _ffn(xs, m, nw[2], ffn_w_in, ffn_w_out, l=l, s=1, mod0=6, rows_per_group=seq, tm=tm_x)

        if not last:
            oc_na = _ctx_na_attn(pc, cols, batch=batch, ctx=n_ctx)
            qc_mla = _mla_q(pc, cos, sin, qn, wuq, cols, n_q=n_ctx, latent=False)
            oc_mla = _mla_attn(qc_mla, pc, px, cos, sin, kvn, wukv, cols,
                               batch=batch, n_q=n_ctx, ctx=n_ctx, seq=seq, latent=False)
            oc_diff = _diff_attn(pc, pc, px, cos, sin, diff_lambda[l], subln, cols,
                                 batch=batch, n_q=n_ctx, ctx=n_ctx, seq=seq, latent=False, lambda_init=lambda_init)
            yc = _merge(oc_na, oc_mla, oc_diff, pc, wb_na, wb_mla, wb_diff, cols, tm=tm_c)
            xc = _outproj(xc, yc, mc[:, 5:6], w_out, l=l, rows_per_group=tm_c, tm=tm_c)
            xc = _ffn_fused(xc, mc, nw[2], ffn_w_in, ffn_w_out, l=l, s=1, mod0=6)

    return _final_norm(xs, final_norm.reshape(1, d), tm=tm_x).reshape(batch, seq, d)
```
